```python
import math
import jax, jax.numpy as jnp
from jax import lax
import numpy as np

D_MODEL = 1024
BATCH = 8
SEQ = 2048
DEPTH = 1
DEC_BATCH = 32
DEC_SEQ = 8
PAST_LEN = 8192
PAGE_SIZE = 128

M_HEADS = 4
M_HEAD_DIM = 128
M_WIDTH = M_HEADS * M_HEAD_DIM
M_CHUNK = 64
CONV_WIDTH = 4
QK_WIDTH = 2 * M_WIDTH
A_HEADS = 4
A_HEAD_DIM = 128
A_GROUP_WIDTH = A_HEADS * A_HEAD_DIM
WINDOWS = (128, 512, 2048)
DILATIONS = (1, 4, 16)
N_GROUPS = 3
A_OUT_WIDTH = A_GROUP_WIDTH
IN_SPLITS = (QK_WIDTH, M_WIDTH, M_WIDTH, M_HEADS, M_HEADS) + (A_GROUP_WIDTH,) * (3 * N_GROUPS) + (D_MODEL, D_MODEL)
D_IN = sum(IN_SPLITS)
P_HEADS = 8
P_QDIM = 256
P_HALF = P_QDIM // 2
P_NKEYS = 128
P_EXPERTS = P_NKEYS * P_NKEYS
P_TOPK = 16
P_TOKEN_BLOCK = 256
RMS_EPS = 1e-6

kernel_name = 'hybrid_mlstm_dilated_attn_peer_step'


def _rmsnorm(x, g):
    x32 = x.astype(jnp.float32)
    y = x32 * lax.rsqrt(jnp.mean(x32 * x32, axis=-1, keepdims=True) + RMS_EPS)
    return (y * g.astype(jnp.float32)).astype(x.dtype)


def _softmax_lse(s):
    m = jnp.max(s, axis=-1, keepdims=True)
    e = jnp.exp(s - m)
    l = jnp.sum(e, axis=-1, keepdims=True)
    return e / l, (m + jnp.log(l))[..., 0]


def _causal_conv(u, state, w, b):
    L = u.shape[1]
    uu = jnp.concatenate([state.astype(u.dtype), u], axis=1)
    out = b + w[0] * uu[:, 0:L]
    for j in range(1, CONV_WIDTH):
        out = out + w[j] * uu[:, j:j + L]
    return out, uu[:, L:]


def _mlstm_chunkwise(q, k, v, ipre, logf, C0, n0, m0):
    B, H, L, _ = q.shape
    lc = math.gcd(L, M_CHUNK)
    nc = L // lc

    def chunks(t):
        return jnp.moveaxis(t.reshape(t.shape[:2] + (nc, lc) + t.shape[3:]), 2, 0)

    causal = jnp.tril(jnp.ones((lc, lc), dtype=bool))

    def step(carry, inp):
        C, n, m = carry
        qc, kc, vc, ic, fc = inp
        b = jnp.cumsum(fc, axis=-1)
        dmat = b[..., :, None] - b[..., None, :] + ic[..., None, :]
        dmat = jnp.where(causal, dmat, -jnp.inf)
        inter = b + m[..., None]
        mt = jnp.maximum(inter, jnp.max(dmat, axis=-1))
        w_in = jnp.exp(dmat - mt[..., None])
        w_prev = jnp.exp(inter - mt)
        s = jnp.einsum('bhtd,bhsd->bhts', qc, kc) * w_in
        num = jnp.einsum('bhts,bhsv->bhtv', s, vc) + w_prev[..., None] * jnp.einsum('bhtk,bhkv->bhtv', qc, C)
        den = jnp.sum(s, axis=-1) + w_prev * jnp.einsum('bhtk,bhk->bht', qc, n)
        h = num / jnp.maximum(jnp.abs(den), jnp.exp(-mt))[..., None]
        m_new = mt[..., -1]
        w_end = jnp.exp(b[..., -1:] - b + ic - m_new[..., None])
        decay = jnp.exp(b[..., -1] + m - m_new)
        C_new = decay[..., None, None] * C + jnp.einsum('bhs,bhsk,bhsv->bhkv', w_end, kc, vc)
        n_new = decay[..., None] * n + jnp.einsum('bhs,bhsk->bhk', w_end, kc)
        return (C_new, n_new, m_new), h

    (C, n, m), hs = lax.scan(step, (C0, n0, m0),
                             (chunks(q), chunks(k), chunks(v), chunks(ipre), chunks(logf)))
    h = jnp.moveaxis(hs, 0, 2).reshape(B, H, L, v.shape[-1])
    return h, C, n, m


def _band_dilated_attn(q, k, v, window, dil):
    B, S, H, hd = q.shape
    nw = window // dil
    ls = S // dil
    blk = min(nw, ls)
    nb = -(-ls // blk)
    lp = nb * blk

    def sub(t, front):
        t = t.astype(jnp.float32).reshape(B, ls, dil, H, hd)
        return jnp.pad(t, ((0, 0), (front, lp - ls), (0, 0), (0, 0), (0, 0)))

    qs = sub(q, 0).reshape(B, nb, blk, dil, H, hd)
    ks = sub(k, blk).reshape(B, nb + 1, blk, dil, H, hd)
    vs = sub(v, blk).reshape(B, nb + 1, blk, dil, H, hd)
    kb = jnp.concatenate([ks[:, :-1], ks[:, 1:]], axis=2)
    vb = jnp.concatenate([vs[:, :-1], vs[:, 1:]], axis=2)
    s = jnp.einsum('bnqrhd,bnkrhd->bnrhqk', qs, kb) * (hd ** -0.5)
    qi = jnp.arange(blk)[:, None] + blk
    ki = jnp.arange(2 * blk)[None, :]
    dist = qi - ki
    kpos = jnp.arange(nb)[:, None, None] * blk + ki[None] - blk
    mask = (dist >= 0) & (dist <= nw) & (kpos >= 0)
    s = jnp.where(mask[None, :, None, None], s, -jnp.inf)
    p, lse = _softmax_lse(s)
    o = jnp.einsum('bnrhqk,bnkrhd->bnqrhd', p, vb)
    o = o.reshape(B, lp, dil, H, hd)[:, :ls].reshape(B, S, H, hd)
    lse = lse.transpose(0, 1, 4, 2, 3).reshape(B, lp, dil, H)[:, :ls].reshape(B, S, H)
    return o, lse


def _gathered_dilated_attn(q, k, v, buf, window, dil):
    T = q.shape[1]
    Wb = buf.shape[1]
    nw = window // dil
    ek = jnp.concatenate([buf[:, :, 0].astype(k.dtype), k], axis=1)
    ev = jnp.concatenate([buf[:, :, 1].astype(v.dtype), v], axis=1)
    idx = Wb + jnp.arange(T)[:, None] - dil * jnp.arange(nw + 1)[None, :]
    valid = idx >= 0
    idx = jnp.maximum(idx, 0)
    kg = ek[:, idx].astype(jnp.float32)
    vg = ev[:, idx].astype(jnp.float32)
    s = jnp.einsum('bthd,btjhd->bthj', q.astype(jnp.float32), kg) * (q.shape[-1] ** -0.5)
    s = jnp.where(valid[None, :, None, :], s, -jnp.inf)
    p, lse = _softmax_lse(s)
    o = jnp.einsum('bthj,btjhd->bthd', p, vg)
    new_buf = jnp.stack([ek[:, -Wb:], ev[:, -Wb:]], axis=2)
    return o, lse, new_buf


def _peer(h, wq, k1, k2, u, v):
    T = h.shape[0]
    blk = math.gcd(T, P_TOKEN_BLOCK)

    def one_block(hb):
        q = (hb @ wq).reshape(blk, P_HEADS, P_QDIM).astype(jnp.float32)
        s1 = jnp.einsum('thd,nd->thn', q[..., :P_HALF], k1.astype(jnp.float32))
        s2 = jnp.einsum('thd,nd->thn', q[..., P_HALF:], k2.astype(jnp.float32))
        v1, i1 = lax.top_k(s1, P_TOPK)
        v2, i2 = lax.top_k(s2, P_TOPK)
        cand = (v1[..., :, None] + v2[..., None, :]).reshape(blk, P_HEADS, P_TOPK * P_TOPK)
        cidx = (i1[..., :, None] * P_NKEYS + i2[..., None, :]).reshape(blk, P_HEADS, P_TOPK * P_TOPK)
        sc, pos = lax.top_k(cand, P_TOPK)
        eidx = jnp.take_along_axis(cidx, pos, axis=-1)
        g = jax.nn.softmax(sc, axis=-1)
        act = jax.nn.gelu(jnp.einsum('td,thkd->thk', hb, u[eidx]).astype(jnp.float32), approximate=False)
        return jnp.einsum('thk,thkd->td', (g * act).astype(hb.dtype), v[eidx])

    out = lax.map(one_block, h.reshape(T // blk, blk, h.shape[-1]))
    return out.reshape(T, h.shape[-1])


def _layer(x, c, state, prm):
    B, L, _ = x.shape
    dt = x.dtype
    f32 = jnp.float32
    prompt = state is None
    if prompt:
        bufs = (None,) * N_GROUPS
        C0 = jnp.zeros((B, M_HEADS, M_HEAD_DIM, M_HEAD_DIM), f32)
        n0 = jnp.zeros((B, M_HEADS, M_HEAD_DIM), f32)
        m0 = jnp.zeros((B, M_HEADS), f32)
        conv_s = jnp.zeros((B, CONV_WIDTH - 1, QK_WIDTH), dt)
    else:
        bufs = state[:N_GROUPS]
        C0 = state[3].astype(f32)
        n0 = state[4].astype(f32)
        m0 = state[5].astype(f32)
        conv_s = state[6]

    mod = (jax.nn.silu(c) @ prm['w_ada'] + prm['b_ada'])[:, None, :]
    sh1, sc1, ga1, sh2, sc2, ga2 = jnp.split(mod, 6, axis=-1)

    h = _rmsnorm(x, prm['g_pre1']) * (1 + sc1) + sh1
    split_points = np.cumsum(IN_SPLITS)[:-1].tolist()
    parts = jnp.split(h @ prm['w_in'], split_points, axis=-1)

    qk, conv_new = _causal_conv(parts[0], conv_s, prm['conv_w'], prm['conv_b'])
    qk = jax.nn.silu(qk)

    def heads(t):
        return t.reshape(B, L, M_HEADS, M_HEAD_DIM).transpose(0, 2, 1, 3).astype(f32)

    qm = heads(qk[..., :M_WIDTH])
    km = heads(qk[..., M_WIDTH:]) * (M_HEAD_DIM ** -0.5)
    vm = heads(parts[1])
    ipre = (parts[3] + prm['b_igate']).astype(f32).transpose(0, 2, 1)
    logf = jax.nn.log_sigmoid((parts[4] + prm['b_fgate']).astype(f32)).transpose(0, 2, 1)
    hm, C1, n1, m1 = _mlstm_chunkwise(qm, km, vm, ipre, logf, C0, n0, m0)
    hm = hm.transpose(0, 2, 1, 3) * jax.nn.sigmoid(parts[2].astype(f32)).reshape(B, L, M_HEADS, M_HEAD_DIM)
    hm = hm * lax.rsqrt(jnp.mean(hm * hm, axis=-1, keepdims=True) + RMS_EPS)
    hm = (hm.reshape(B, L, M_WIDTH) * prm['mh_norm_g'].astype(f32)).astype(dt)
    branch_a = hm @ prm['w_a']

    outs, lses, new_bufs = [], [], []
    for g in range(N_GROUPS):
        qa, ka, va = (t.reshape(B, L, A_HEADS, A_HEAD_DIM) for t in parts[5 + 3 * g:8 + 3 * g])
        if prompt:
            o, lse = _band_dilated_attn(qa, ka, va, WINDOWS[g], DILATIONS[g])
            keep = min(WINDOWS[g], L)
            nbuf = jnp.stack([ka[:, L - keep:], va[:, L - keep:]], axis=2)
        else:
            o, lse, nbuf = _gathered_dilated_attn(qa, ka, va, bufs[g], WINDOWS[g], DILATIONS[g])
        outs.append(o)
        lses.append(lse)
        new_bufs.append(nbuf)
    wgt = jax.nn.softmax(jnp.stack(lses), axis=0)
    oa = jnp.einsum('gblh,gblhd->blhd', wgt, jnp.stack(outs)).reshape(B, L, A_OUT_WIDTH).astype(dt)
    branch_b = oa @ prm['w_b']

    merged = jax.nn.sigmoid(parts[14]) * branch_a + jax.nn.sigmoid(parts[15]) * branch_b
    x = x + ga1 * _rmsnorm(merged @ prm['w_out'], prm['g_post1'])

    h2 = _rmsnorm(x, prm['g_pre2']) * (1 + sc2) + sh2
    f = _peer(h2.reshape(B * L, D_MODEL), prm['peer_wq'], prm['peer_k1'], prm['peer_k2'],
              prm['peer_u'], prm['peer_v']).reshape(B, L, D_MODEL)
    x = x + ga2 * _rmsnorm(f, prm['g_post2'])
    return x, (new_bufs[0], new_bufs[1], new_bufs[2], C1, n1, m1, conv_new)


def setup_inputs(seed: int = 0) -> dict:
    key = jax.random.key(seed)
    ks = jax.random.split(key, 40)
    f32 = jnp.float32

    def nrm(k, shape, scale):
        return jax.random.normal(k, shape, f32) * scale

    def gain(k, shape):
        return 1.0 + 0.02 * jax.random.normal(k, shape, f32)

    D = D_MODEL
    inp = {}
    inp['x_prompt'] = nrm(ks[0], (BATCH, SEQ, D), 1.0)
    inp['x_sample'] = nrm(ks[1], (DEC_BATCH, DEC_SEQ, D), 1.0)
    for g in range(N_GROUPS):
        inp['cache_win%d_kv' % g] = nrm(ks[2 + g], (DEPTH, DEC_BATCH, min(WINDOWS[g], PAST_LEN), 2, A_HEADS, A_HEAD_DIM), 1.0)
    inp['state_mlstm_C'] = nrm(ks[5], (DEPTH, DEC_BATCH, M_HEADS, M_HEAD_DIM, M_HEAD_DIM), 0.1)
    inp['state_mlstm_n'] = nrm(ks[6], (DEPTH, DEC_BATCH, M_HEADS, M_HEAD_DIM), 0.5)
    inp['state_mlstm_m'] = jax.random.uniform(ks[7], (DEPTH, DEC_BATCH, M_HEADS), f32, 0.0, 2.0)
    inp['state_conv'] = nrm(ks[8], (DEPTH, DEC_BATCH, CONV_WIDTH - 1, QK_WIDTH), 1.0)
    inp['c_prompt'] = nrm(ks[9], (BATCH, D), 1.0)
    inp['c_sample'] = nrm(ks[10], (DEC_BATCH, D), 1.0)
    inp['w_ada'] = nrm(ks[11], (DEPTH, D, 6 * D), 0.5 * D ** -0.5)
    inp['b_ada'] = nrm(ks[12], (DEPTH, 6 * D), 0.02)
    inp['g_pre1'] = gain(ks[13], (DEPTH, D))
    inp['g_post1'] = gain(ks[14], (DEPTH, D))
    inp['g_pre2'] = gain(ks[15], (DEPTH, D))
    inp['g_post2'] = gain(ks[16], (DEPTH, D))
    inp['w_in'] = nrm(ks[17], (DEPTH, D, D_IN), D ** -0.5)
    inp['conv_w'] = nrm(ks[18], (DEPTH, CONV_WIDTH, QK_WIDTH), CONV_WIDTH ** -0.5)
    inp['conv_b'] = nrm(ks[19], (DEPTH, QK_WIDTH), 0.02)
    inp['b_igate'] = nrm(ks[20], (DEPTH, M_HEADS), 0.1)
    inp['b_fgate'] = jnp.linspace(3.0, 6.0, M_HEADS, dtype=f32)[None] + nrm(ks[21], (DEPTH, M_HEADS), 0.1)
    inp['mh_norm_g'] = gain(ks[22], (DEPTH, M_WIDTH))
    inp['w_a'] = nrm(ks[23], (DEPTH, M_WIDTH, D), M_WIDTH ** -0.5)
    inp['w_b'] = nrm(ks[24], (DEPTH, A_OUT_WIDTH, D), A_OUT_WIDTH ** -0.5)
    inp['w_out'] = nrm(ks[25], (DEPTH, D, D), D ** -0.5)
    inp['peer_wq'] = nrm(ks[26], (DEPTH, D, P_HEADS * P_QDIM), D ** -0.5)
    inp['peer_k1'] = nrm(ks[27], (DEPTH, P_NKEYS, P_HALF), P_HALF ** -0.5)
    inp['peer_k2'] = nrm(ks[28], (DEPTH, P_NKEYS, P_HALF), P_HALF ** -0.5)
    inp['peer_u'] = nrm(ks[29], (DEPTH, P_EXPERTS, D), D ** -0.5)
    inp['peer_v'] = nrm(ks[30], (DEPTH, P_EXPERTS, D), (P_HEADS * P_TOPK) ** -0.5)
    return inp


def reference(x_prompt, x_sample, cache_win0_kv, cache_win1_kv, cache_win2_kv,
              state_mlstm_C, state_mlstm_n, state_mlstm_m, state_conv,
              c_prompt, c_sample, w_ada, b_ada, g_pre1, g_post1, g_pre2, g_post2,
              w_in, conv_w, conv_b, b_igate, b_fgate, mh_norm_g, w_a, w_b, w_out,
              peer_wq, peer_k1, peer_k2, peer_u, peer_v):
    yp, ys = x_prompt, x_sample
    new_p, new_s = [], []
    for layer in range(DEPTH):
        prm = {
            'w_ada': w_ada[layer], 'b_ada': b_ada[layer],
            'g_pre1': g_pre1[layer], 'g_post1': g_post1[layer],
            'g_pre2': g_pre2[layer], 'g_post2': g_post2[layer],
            'w_in': w_in[layer], 'conv_w': conv_w[layer], 'conv_b': conv_b[layer],
            'b_igate': b_igate[layer], 'b_fgate': b_fgate[layer], 'mh_norm_g': mh_norm_g[layer],
            'w_a': w_a[layer], 'w_b': w_b[layer], 'w_out': w_out[layer],
            'peer_wq': peer_wq[layer], 'peer_k1': peer_k1[layer], 'peer_k2': peer_k2[layer],
            'peer_u': peer_u[layer], 'peer_v': peer_v[layer],
        }
        yp, sp = _layer(yp, c_prompt, None, prm)
        ys, ss = _layer(ys, c_sample, (cache_win0_kv[layer], cache_win1_kv[layer], cache_win2_kv[layer],
                                       state_mlstm_C[layer], state_mlstm_n[layer], state_mlstm_m[layer],
                                       state_conv[layer]), prm)
        new_p.append(sp)
        new_s.append(ss)
    win0_p, win1_p, win2_p, C_p, n_p, m_p, conv_p = (jnp.stack(z) for z in zip(*new_p))
    win0_s, win1_s, win2_s, C_s, n_s, m_s, conv_s = (jnp.stack(z) for z in zip(*new_s))
    return (yp, ys, win0_p, win1_p, win2_p, C_p, n_p, m_p, conv_p,
            win0_s, win1_s, win2_s, C_s, n_s, m_s, conv_s)
```

```python
import functools
import math

import jax
import jax.numpy as jnp
from jax import lax
from jax.experimental import pallas as pl
from jax.experimental.pallas import tpu as pltpu

F32 = jnp.float32
BF16 = jnp.bfloat16
HIGHEST = lax.Precision.HIGHEST

RMS_EPS = 1e-6
M_HEADS = 4
M_HEAD_DIM = 128
M_WIDTH = M_HEADS * M_HEAD_DIM
CONV_WIDTH = 4
QK_WIDTH = 2 * M_WIDTH
A_HEADS = 4
A_HEAD_DIM = 128
A_WIDTH = A_HEADS * A_HEAD_DIM
WINDOWS = (128, 512, 2048)
DILATIONS = (1, 4, 16)
N_GROUPS = 3
P_HEADS = 8
P_QDIM = 256
P_HALF = P_QDIM // 2
P_NKEYS = 128
P_TOPK = 16

LANE = 128
GATE_PAD = LANE
MLSTM_CHUNK = 256
ATTN_BLOCK = 128
VMEM_LIMIT = 48 * 1024 * 1024

_NT = (((1,), (1,)), ((), ()))
_TN = (((0,), (0,)), ((), ()))


def _params(*sem):
    return pltpu.CompilerParams(dimension_semantics=sem, vmem_limit_bytes=VMEM_LIMIT)


def _rms(x):
    return x * lax.rsqrt(jnp.mean(x * x, axis=-1, keepdims=True) + RMS_EPS)


def _adaln_kernel(c_ref, w_ref, b_ref, o_ref):
    c = c_ref[...]
    a = (c * jax.nn.sigmoid(c)).astype(BF16)
    o_ref[...] = jnp.dot(a, w_ref[...].astype(BF16), preferred_element_type=F32) + b_ref[...]


def _adaln(c_all, w_ada, b_ada):
    bc, d = c_all.shape
    n = w_ada.shape[1]
    tn = 1024
    return pl.pallas_call(
        _adaln_kernel,
        grid=(n // tn,),
        in_specs=[pl.BlockSpec((bc, d), lambda j: (0, 0)),
                  pl.BlockSpec((d, tn), lambda j: (0, j)),
                  pl.BlockSpec((1, tn), lambda j: (0, j))],
        out_specs=pl.BlockSpec((bc, tn), lambda j: (0, j)),
        out_shape=jax.ShapeDtypeStruct((bc, n), F32),
        compiler_params=_params("parallel"),
        name="adaln",
    )(c_all, w_ada, b_ada.reshape(1, n))


def _prenorm_kernel(x_ref, sh_ref, sc_ref, g_ref, o_ref):
    y = _rms(x_ref[...]) * g_ref[...]
    o_ref[...] = (y * (1.0 + sc_ref[...]) + sh_ref[...]).astype(o_ref.dtype)


def _prenorm(x, shift, scale, g, bb, lt):
    b, l, d = x.shape
    row = pl.BlockSpec((bb, lt, d), lambda i, j: (i, j, 0))
    vec = pl.BlockSpec((bb, 1, d), lambda i, j: (i, 0, 0))
    return pl.pallas_call(
        _prenorm_kernel,
        grid=(b // bb, l // lt),
        in_specs=[row, vec, vec, pl.BlockSpec((1, 1, d), lambda i, j: (0, 0, 0))],
        out_specs=row,
        out_shape=jax.ShapeDtypeStruct((b, l, d), BF16),
        compiler_params=_params("parallel", "parallel"),
        name="prenorm",
    )(x, shift, scale, g.reshape(1, 1, d))


def _mm_kernel(n, a_ref, *refs):
    a = a_ref[...]
    for w_ref, o_ref in zip(refs[:n], refs[n:]):
        o_ref[...] = jnp.dot(a, w_ref[...], preferred_element_type=F32).astype(o_ref.dtype)


def _multi_matmul(a, ws, tm, name):
    m, k = a.shape
    n = len(ws)
    return pl.pallas_call(
        functools.partial(_mm_kernel, n),
        grid=(m // tm,),
        in_specs=[pl.BlockSpec((tm, k), lambda i: (i, 0))]
        + [pl.BlockSpec(w.shape, lambda i: (0, 0)) for w in ws],
        out_specs=[pl.BlockSpec((tm, w.shape[1]), lambda i: (i, 0)) for w in ws],
        out_shape=[jax.ShapeDtypeStruct((m, w.shape[1]), F32) for w in ws],
        compiler_params=_params("parallel"),
        name=name,
    )(a, *ws)


def _log_sigmoid(x):
    return jnp.minimum(x, 0.0) - jnp.log1p(jnp.exp(-jnp.abs(x)))


def _mlstm_kernel(has_state, c, mp_ref, cw_ref, cb_ref, gb_ref, mg_ref, *refs):
    if has_state:
        c0_ref, n0_ref, m0_ref, cv0_ref = refs[:4]
        refs = refs[4:]
    hm_ref, c_ref, n_ref, m_ref, cv_ref, ubuf = refs
    j = pl.program_id(1)
    hd = M_HEAD_DIM
    pad = 8

    @pl.when(j == 0)
    def _init():
        if has_state:
            c_ref[...] = c0_ref[...]
            n_ref[...] = n0_ref[...]
            m_ref[...] = m0_ref[...]
            ubuf[pad - 3:pad, :] = cv0_ref[0]
        else:
            c_ref[...] = jnp.zeros_like(c_ref)
            n_ref[...] = jnp.zeros_like(n_ref)
            m_ref[...] = jnp.zeros_like(m_ref)
            ubuf[pad - 3:pad, :] = jnp.zeros((3, QK_WIDTH), F32)

    ubuf[pad:pad + c, :] = mp_ref[0, :, 0:QK_WIDTH]
    acc = cb_ref[...] + cw_ref[0:1, :] * ubuf[pad - 3:pad - 3 + c, :]
    for t in range(1, CONV_WIDTH):
        acc = acc + cw_ref[t:t + 1, :] * ubuf[pad - 3 + t:pad - 3 + t + c, :]
    tail = ubuf[pad + c - 3:pad + c, :]
    ubuf[pad - 3:pad, :] = tail
    cv_ref[0] = tail
    qk = acc * jax.nn.sigmoid(acc)

    lane = lax.broadcasted_iota(jnp.int32, (c, GATE_PAD), 1)
    graw = mp_ref[0, :, QK_WIDTH + 2 * M_WIDTH:QK_WIDTH + 2 * M_WIDTH + GATE_PAD] + gb_ref[...]
    gates = jnp.where(lane < M_HEADS, graw, jnp.where(lane < 2 * M_HEADS, _log_sigmoid(graw), 0.0))
    row = lax.broadcasted_iota(jnp.int32, (c, c), 0)
    col = lax.broadcasted_iota(jnp.int32, (c, c), 1)
    causal = col <= row
    tril = jnp.where(causal, 1.0, 0.0).astype(F32)
    cum = jnp.dot(tril, gates, precision=HIGHEST, preferred_element_type=F32)

    for h in range(M_HEADS):
        sl = slice(h * hd, (h + 1) * hd)
        qh = qk[:, sl].astype(BF16)
        kh = qk[:, M_WIDTH + h * hd:M_WIDTH + (h + 1) * hd] * (hd ** -0.5)
        vh = mp_ref[0, :, QK_WIDTH + h * hd:QK_WIDTH + (h + 1) * hd].astype(BF16)
        og = mp_ref[0, :, QK_WIDTH + M_WIDTH + h * hd:QK_WIDTH + M_WIDTH + (h + 1) * hd]
        i_col = gates[:, h:h + 1]
        b_col = cum[:, M_HEADS + h:M_HEADS + h + 1]
        lhs = jnp.where(lane == 0, b_col, jnp.where(lane == 1, 1.0, 0.0))
        rhs = jnp.where(lane == 0, 1.0, jnp.where(lane == 1, i_col - b_col, 0.0))
        dmat = lax.dot_general(lhs, rhs, _NT, precision=HIGHEST, preferred_element_type=F32)
        dmat = jnp.where(causal, dmat, -jnp.inf)
        m_prev = m_ref[0, h:h + 1, 0:1]
        inter = b_col + m_prev
        mt = jnp.maximum(inter, jnp.max(dmat, axis=-1, keepdims=True))
        w_in = jnp.exp(dmat - mt)
        w_prev = jnp.exp(inter - mt)
        s = lax.dot_general(qh, kh.astype(BF16), _NT, preferred_element_type=F32) * w_in
        cmat = c_ref[0, h]
        nrow = n_ref[0, h:h + 1, :]
        num = jnp.dot(s.astype(BF16), vh, preferred_element_type=F32) + w_prev * jnp.dot(
            qh, cmat.astype(BF16), preferred_element_type=F32)
        den = jnp.sum(s, axis=-1, keepdims=True) + w_prev * jnp.sum(
            qk[:, sl] * nrow, axis=-1, keepdims=True)
        hh = num / jnp.maximum(jnp.abs(den), jnp.exp(-mt))
        m_new = mt[c - 1:c, :]
        b_last = b_col[c - 1:c, :]
        w_end = jnp.exp(b_last - b_col + i_col - m_new)
        decay = jnp.exp(b_last + m_prev - m_new)
        kw = kh * w_end
        c_ref[0, h] = decay * cmat + lax.dot_general(kw.astype(BF16), vh, _TN, preferred_element_type=F32)
        n_ref[0, h:h + 1, :] = decay * nrow + jnp.sum(kw, axis=0, keepdims=True)
        m_ref[0, h:h + 1, :] = jnp.broadcast_to(m_new, (1, LANE))
        hh = hh * jax.nn.sigmoid(og)
        hm_ref[0, :, sl] = _rms(hh) * mg_ref[:, sl]


def _mlstm(mparts, conv_w, conv_b, gate_b, mh_g, state, c):
    b, l, width = mparts.shape
    has_state = state is not None
    nj = l // c
    per_b4 = lambda i, j: (i, 0, 0, 0)
    per_b3 = lambda i, j: (i, 0, 0)
    const2 = lambda i, j: (0, 0)
    in_specs = [pl.BlockSpec((1, c, width), lambda i, j: (i, j, 0)),
                pl.BlockSpec(conv_w.shape, const2),
                pl.BlockSpec((1, QK_WIDTH), const2),
                pl.BlockSpec((1, GATE_PAD), const2),
                pl.BlockSpec((1, M_WIDTH), const2)]
    args = [mparts, conv_w, conv_b.reshape(1, QK_WIDTH), gate_b, mh_g.reshape(1, M_WIDTH)]
    state_specs = [pl.BlockSpec((1, M_HEADS, M_HEAD_DIM, M_HEAD_DIM), per_b4),
                   pl.BlockSpec((1, M_HEADS, M_HEAD_DIM), per_b3),
                   pl.BlockSpec((1, M_HEADS, LANE), per_b3),
                   pl.BlockSpec((1, CONV_WIDTH - 1, QK_WIDTH), per_b3)]
    if has_state:
        in_specs += state_specs
        args += list(state)
    return pl.pallas_call(
        functools.partial(_mlstm_kernel, has_state, c),
        grid=(b, nj),
        in_specs=in_specs,
        out_specs=[pl.BlockSpec((1, c, M_WIDTH), lambda i, j: (i, j, 0))] + state_specs,
        out_shape=[jax.ShapeDtypeStruct((b, l, M_WIDTH), F32),
                   jax.ShapeDtypeStruct((b, M_HEADS, M_HEAD_DIM, M_HEAD_DIM), F32),
                   jax.ShapeDtypeStruct((b, M_HEADS, M_HEAD_DIM), F32),
                   jax.ShapeDtypeStruct((b, M_HEADS, LANE), F32),
                   jax.ShapeDtypeStruct((b, CONV_WIDTH - 1, QK_WIDTH), F32)],
        scratch_shapes=[pltpu.VMEM((8 + c, QK_WIDTH), F32)],
        compiler_params=_params("parallel", "arbitrary"),
        name="mlstm",
    )(*args)


def _attn_prompt_kernel(use_prev, q_ref, kc_ref, kp_ref, o_ref, l_ref):
    n = pl.program_id(2)
    blk, hd = ATTN_BLOCK, A_HEAD_DIM
    scale = hd ** -0.5
    row = lax.broadcasted_iota(jnp.int32, (blk, blk), 0)
    col = lax.broadcasted_iota(jnp.int32, (blk, blk), 1)
    cur_ok = col <= row
    prev_ok = (col - row) >= jnp.where(n > 0, 0, blk)
    for h in range(A_HEADS):
        sl = slice(h * hd, (h + 1) * hd)
        sv = slice(A_WIDTH + h * hd, A_WIDTH + (h + 1) * hd)
        q = q_ref[0, :, sl].astype(BF16)
        s_c = lax.dot_general(q, kc_ref[0, :, sl].astype(BF16), _NT, preferred_element_type=F32) * scale
        s_c = jnp.where(cur_ok, s_c, -jnp.inf)
        m = jnp.max(s_c, axis=-1, keepdims=True)
        if use_prev:
            s_p = lax.dot_general(q, kp_ref[0, :, sl].astype(BF16), _NT, preferred_element_type=F32) * scale
            s_p = jnp.where(prev_ok, s_p, -jnp.inf)
            m = jnp.maximum(m, jnp.max(s_p, axis=-1, keepdims=True))
        e_c = jnp.exp(s_c - m)
        den = jnp.sum(e_c, axis=-1, keepdims=True)
        acc = jnp.dot(e_c.astype(BF16), kc_ref[0, :, sv].astype(BF16), preferred_element_type=F32)
        if use_prev:
            e_p = jnp.exp(s_p - m)
            den = den + jnp.sum(e_p, axis=-1, keepdims=True)
            acc = acc + jnp.dot(e_p.astype(BF16), kp_ref[0, :, sv].astype(BF16), preferred_element_type=F32)
        o_ref[0, :, sl] = acc / den
        l_ref[0, :, sl] = jnp.broadcast_to(m + jnp.log(den), (blk, hd))


def _attn_prompt(q_all, kv, g):
    b, s, _ = q_all.shape
    dil = DILATIONS[g]
    assert WINDOWS[g] // dil == ATTN_BLOCK and s % (dil * ATTN_BLOCK) == 0
    ls = s // dil
    nb = ls // ATTN_BLOCK
    qv = q_all.reshape(b, ls, dil * N_GROUPS * A_WIDTH)
    kvv = kv.reshape(b, ls, dil * 2 * A_WIDTH)
    o_spec = pl.BlockSpec((1, ATTN_BLOCK, A_WIDTH), lambda i, r, n: (i, n, r))
    o, lse = pl.pallas_call(
        functools.partial(_attn_prompt_kernel, nb > 1),
        grid=(b, dil, nb),
        in_specs=[pl.BlockSpec((1, ATTN_BLOCK, A_WIDTH), lambda i, r, n: (i, n, r * N_GROUPS + g)),
                  pl.BlockSpec((1, ATTN_BLOCK, 2 * A_WIDTH), lambda i, r, n: (i, n, r)),
                  pl.BlockSpec((1, ATTN_BLOCK, 2 * A_WIDTH), lambda i, r, n: (i, jnp.maximum(n - 1, 0), r))],
        out_specs=[o_spec, o_spec],
        out_shape=[jax.ShapeDtypeStruct((b, ls, dil * A_WIDTH), F32)] * 2,
        compiler_params=_params("parallel", "parallel", "arbitrary"),
        name="attn_prompt_g%d" % g,
    )(qv, kvv, kvv)
    return o.reshape(b, s, A_WIDTH), lse.reshape(b, s, A_WIDTH)


def _attn_sample_kernel(window, dil, q_ref, kvn_ref, buf_ref, o_ref, l_ref, nbuf_ref):
    t_new = q_ref.shape[1]
    wb = buf_ref.shape[1]
    hd = A_HEAD_DIM
    scale = hd ** -0.5
    tb = lax.broadcasted_iota(jnp.int32, (t_new, wb), 0)
    eb = lax.broadcasted_iota(jnp.int32, (t_new, wb), 1)
    dist_b = wb + tb - eb
    ok_b = (dist_b <= window) & ((dist_b & (dil - 1)) == 0)
    tn = lax.broadcasted_iota(jnp.int32, (t_new, t_new), 0)
    en = lax.broadcasted_iota(jnp.int32, (t_new, t_new), 1)
    dist_n = tn - en
    ok_n = (dist_n >= 0) & (dist_n <= window) & ((dist_n & (dil - 1)) == 0)
    for h in range(A_HEADS):
        sl = slice(h * hd, (h + 1) * hd)
        sv = slice(A_WIDTH + h * hd, A_WIDTH + (h + 1) * hd)
        q = q_ref[0, :, sl].astype(BF16)
        s_b = lax.dot_general(q, buf_ref[0, :, sl].astype(BF16), _NT, preferred_element_type=F32) * scale
        s_n = lax.dot_general(q, kvn_ref[0, :, sl].astype(BF16), _NT, preferred_element_type=F32) * scale
        s_b = jnp.where(ok_b, s_b, -jnp.inf)
        s_n = jnp.where(ok_n, s_n, -jnp.inf)
        m = jnp.maximum(jnp.max(s_b, axis=-1, keepdims=True), jnp.max(s_n, axis=-1, keepdims=True))
        e_b = jnp.exp(s_b - m)
        e_n = jnp.exp(s_n - m)
        den = jnp.sum(e_b, axis=-1, keepdims=True) + jnp.sum(e_n, axis=-1, keepdims=True)
        acc = jnp.dot(e_b.astype(BF16), buf_ref[0, :, sv].astype(BF16), preferred_element_type=F32)
        acc = acc + jnp.dot(e_n.astype(BF16), kvn_ref[0, :, sv].astype(BF16), preferred_element_type=F32)
        o_ref[0, :, sl] = acc / den
        l_ref[0, :, sl] = jnp.broadcast_to(m + jnp.log(den), (t_new, hd))
    nbuf_ref[0, 0:wb - t_new, :] = buf_ref[0, t_new:wb, :]
    nbuf_ref[0, wb - t_new:wb, :] = kvn_ref[0]


def _attn_sample(q_all, kvn, buf, g):
    b, t_new, _ = q_all.shape
    wb = buf.shape[1]
    assert wb == WINDOWS[g] and wb >= t_new and t_new % 8 == 0
    assert DILATIONS[g] & (DILATIONS[g] - 1) == 0
    row = lambda i: (i, 0, 0)
    o_spec = pl.BlockSpec((1, t_new, A_WIDTH), row)
    return pl.pallas_call(
        functools.partial(_attn_sample_kernel, WINDOWS[g], DILATIONS[g]),
        grid=(b,),
        in_specs=[pl.BlockSpec((1, t_new, A_WIDTH), lambda i: (i, 0, g)),
                  pl.BlockSpec((1, t_new, 2 * A_WIDTH), row),
                  pl.BlockSpec((1, wb, 2 * A_WIDTH), row)],
        out_specs=[o_spec, o_spec, pl.BlockSpec((1, wb, 2 * A_WIDTH), row)],
        out_shape=[jax.ShapeDtypeStruct((b, t_new, A_WIDTH), F32),
                   jax.ShapeDtypeStruct((b, t_new, A_WIDTH), F32),
                   jax.ShapeDtypeStruct((b, wb, 2 * A_WIDTH), F32)],
        compiler_params=_params("parallel"),
        name="attn_sample_g%d" % g,
    )(q_all, kvn, buf)


def _merge_kernel(hm_ref, o0_ref, o1_ref, o2_ref, l0_ref, l1_ref, l2_ref, gate_ref, x_ref,
                  ga1_ref, sh2_ref, sc2_ref, wa_ref, wb_ref, wo_ref, gp1_ref, gp2_ref,
                  x1_ref, h2_ref):
    bb, lt, d = x_ref.shape
    rows = bb * lt
    two = lambda ref: ref[...].reshape(rows, ref.shape[-1])
    l0, l1, l2 = two(l0_ref), two(l1_ref), two(l2_ref)
    m = jnp.maximum(jnp.maximum(l0, l1), l2)
    e0, e1, e2 = jnp.exp(l0 - m), jnp.exp(l1 - m), jnp.exp(l2 - m)
    oa = (e0 * two(o0_ref) + e1 * two(o1_ref) + e2 * two(o2_ref)) / (e0 + e1 + e2)
    br_a = jnp.dot(two(hm_ref).astype(BF16), wa_ref[...], preferred_element_type=F32)
    br_b = jnp.dot(oa.astype(BF16), wb_ref[...], preferred_element_type=F32)
    gate = two(gate_ref)
    merged = jax.nn.sigmoid(gate[:, 0:d]) * br_a + jax.nn.sigmoid(gate[:, d:2 * d]) * br_b
    y = jnp.dot(merged.astype(BF16), wo_ref[...], preferred_element_type=F32)
    y = (_rms(y) * gp1_ref[...]).reshape(bb, lt, d)
    x1 = x_ref[...] + ga1_ref[...] * y
    x1_ref[...] = x1
    h2 = (_rms(x1) * gp2_ref[...].reshape(1, 1, d)) * (1.0 + sc2_ref[...]) + sh2_ref[...]
    h2_ref[...] = h2.reshape(rows, d).astype(h2_ref.dtype)


def _merge(hm, outs, lses, gates, x, ga1, sh2, sc2, w_a, w_b, w_out, g_post1, g_pre2, bb, lt):
    b, l, d = x.shape
    rows = bb * lt
    nl = l // lt
    idx = lambda i, j: (i, j, 0)
    vec = pl.BlockSpec((bb, 1, d), lambda i, j: (i, 0, 0))
    const = lambda i, j: (0, 0)
    aw = pl.BlockSpec((bb, lt, A_WIDTH), idx)
    return pl.pallas_call(
        _merge_kernel,
        grid=(b // bb, nl),
        in_specs=[pl.BlockSpec((bb, lt, M_WIDTH), idx), aw, aw, aw, aw, aw, aw,
                  pl.BlockSpec((bb, lt, 2 * d), idx), pl.BlockSpec((bb, lt, d), idx),
                  vec, vec, vec,
                  pl.BlockSpec(w_a.shape, const), pl.BlockSpec(w_b.shape, const),
                  pl.BlockSpec(w_out.shape, const),
                  pl.BlockSpec((1, d), const), pl.BlockSpec((1, d), const)],
        out_specs=[pl.BlockSpec((bb, lt, d), idx),
                   pl.BlockSpec((rows, d), lambda i, j: (i * nl + j, 0))],
        out_shape=[jax.ShapeDtypeStruct((b, l, d), F32),
                   jax.ShapeDtypeStruct((b * l, d), BF16)],
        compiler_params=_params("parallel", "parallel"),
        name="merge",
    )(hm, *outs, *lses, gates, x, ga1, sh2, sc2, w_a, w_b, w_out,
      g_post1.reshape(1, d), g_pre2.reshape(1, d))


def _topk_rows(x, k):
    n, m = x.shape
    ridx = lax.broadcasted_iota(jnp.int32, (n, m), 0).astype(F32)
    kidx = lax.broadcasted_iota(jnp.int32, (k, m), 0)

    def body(r, carry):
        x, rank, vals = carry
        mx = jnp.max(x, axis=0, keepdims=True)
        first = jnp.min(jnp.where(x == mx, ridx, float(n)), axis=0, keepdims=True)
        sel = ridx == first
        rank = jnp.where(sel, r.astype(F32), rank)
        x = jnp.where(sel, -jnp.inf, x)
        vals = jnp.where(kidx == r, mx, vals)
        return x, rank, vals

    _, rank, vals = lax.fori_loop(
        0, k, body, (x, jnp.full((n, m), float(k), F32), jnp.zeros((k, m), F32)))
    return vals, rank


def _peer_select_kernel(h2_ref, wqt_ref, k1_ref, k2_ref, cnt_ref, e1_ref, rk2_ref, e2_ref, qt_s):
    m = h2_ref.shape[0]
    k = P_TOPK
    qt_s[...] = lax.dot_general(wqt_ref[...], h2_ref[...], _NT, preferred_element_type=F32)
    k1 = k1_ref[...].astype(BF16)
    k2 = k2_ref[...].astype(BF16)
    i16 = lax.broadcasted_iota(jnp.int32, (k, m), 0)
    i8 = lax.broadcasted_iota(jnp.int32, (8, m), 0)
    pos = [(i16 * k).astype(F32), (i8 * k + 1).astype(F32)]
    pos += [(i8 * k + jj).astype(F32) for jj in range(2, 8)]
    pos += [(i8 + 8).astype(F32)]
    pos = jnp.concatenate(pos, axis=0)
    ncand = pos.shape[0]

    def per_head(h, _):
        qa = qt_s[pl.ds(pl.multiple_of(h * P_QDIM, P_QDIM), P_HALF), :].astype(BF16)
        qb = qt_s[pl.ds(pl.multiple_of(h * P_QDIM + P_HALF, P_HALF), P_HALF), :].astype(BF16)
        s1 = jnp.dot(k1, qa, preferred_element_type=F32)
        s2 = jnp.dot(k2, qb, preferred_element_type=F32)
        v1, rk1 = _topk_rows(s1, k)
        v2, rk2 = _topk_rows(s2, k)
        blocks = [v1 + v2[0:1], v1[0:8] + v2[1:2]]
        for jj in range(2, 8):
            blocks.append(jnp.where(i8 < k // (jj + 1), v1[0:8] + v2[jj:jj + 1], -jnp.inf))
        blocks.append(v1[0:1] + v2[8:16])
        cand = jnp.concatenate(blocks, axis=0)
        cmax = v1[0:1] + v2[0:1]

        def pick(r, carry):
            cand, selm, z = carry
            mx = jnp.max(cand, axis=0, keepdims=True)
            first = jnp.min(jnp.where(cand == mx, pos, 1e9), axis=0, keepdims=True)
            sel = pos == first
            selm = jnp.where(sel, 1.0, selm)
            cand = jnp.where(sel, -jnp.inf, cand)
            return cand, selm, z + jnp.exp(mx - cmax)

        _, selm, z = lax.fori_loop(
            0, k, pick, (cand, jnp.zeros((ncand, m), F32), jnp.zeros((1, m), F32)))
        low = selm[16:24]
        for jj in range(2, 8):
            low = low + selm[8 + 8 * jj:16 + 8 * jj]
        cnt16 = selm[0:16] + jnp.concatenate([low, jnp.zeros((8, m), F32)], axis=0)
        cnt16 = cnt16 + jnp.where(i16 == 0, jnp.sum(selm[72:80], axis=0, keepdims=True), 0.0)
        cnt = jnp.zeros_like(s1)
        for r in range(k):
            cnt = cnt + jnp.where(rk1 == float(r), cnt16[r:r + 1], 0.0)
        cnt_ref[h] = cnt
        e1_ref[h] = jnp.exp(s1 - v1[0:1]) / z
        rk2_ref[h] = rk2
        e2_ref[h] = jnp.exp(s2 - v2[0:1])
        return 0

    lax.fori_loop(0, P_HEADS, per_head, 0)


def _peer_select(h2, wq_t, k1, k2):
    t, d = h2.shape
    tm = LANE
    sel_spec = pl.BlockSpec((P_HEADS, P_NKEYS, tm), lambda i: (0, 0, i))
    sel_shape = jax.ShapeDtypeStruct((P_HEADS, P_NKEYS, t), F32)
    return pl.pallas_call(
        _peer_select_kernel,
        grid=(t // tm,),
        in_specs=[pl.BlockSpec((tm, d), lambda i: (i, 0)),
                  pl.BlockSpec(wq_t.shape, lambda i: (0, 0)),
                  pl.BlockSpec(k1.shape, lambda i: (0, 0)),
                  pl.BlockSpec(k2.shape, lambda i: (0, 0))],
        out_specs=[sel_spec] * 4,
        out_shape=[sel_shape] * 4,
        scratch_shapes=[pltpu.VMEM((P_HEADS * P_QDIM, tm), F32)],
        compiler_params=_params("parallel"),
        name="peer_select",
    )(h2, wq_t, k1, k2)


E_SUB = 8
E_TILE = E_SUB * P_NKEYS


def _gelu(x):
    return 0.5 * x * (1.0 + lax.erf(x * math.sqrt(0.5)))


def _peer_dense_kernel(h2_ref, u_ref, vt_ref, cnt_ref, e1_ref, rk2_ref, e2_ref,
                       x1_ref, ga2_ref, gp2_ref, o_ref, s_s, a_s, acc_s):
    kk = pl.program_id(2)
    bb, lt, d = x1_ref.shape
    m = bb * lt

    @pl.when(kk == 0)
    def _zero():
        acc_s[...] = jnp.zeros_like(acc_s)

    s_s[...] = lax.dot_general(u_ref[...], h2_ref[...], _NT, preferred_element_type=F32)
    for i in range(E_SUB):
        rows = slice(i * P_NKEYS, (i + 1) * P_NKEYS)
        w = jnp.zeros((P_NKEYS, m), F32)
        for h in range(P_HEADS):
            picked = rk2_ref[h] < cnt_ref[h, i:i + 1, :]
            w = w + jnp.where(picked, e2_ref[h], 0.0) * e1_ref[h, i:i + 1, :]
        a_s[rows, :] = (w * _gelu(s_s[rows, :])).astype(BF16)
    acc_s[...] += jnp.dot(vt_ref[...], a_s[...], preferred_element_type=F32)

    @pl.when(kk == pl.num_programs(2) - 1)
    def _finish():
        f = acc_s[...].T
        y = (_rms(f) * gp2_ref[...]).reshape(bb, lt, d)
        o_ref[...] = x1_ref[...] + ga2_ref[...] * y


def _peer_dense(h2, u, v_t, sel, x1, ga2, g_post2, bb, lt):
    b, l, d = x1.shape
    cnt, e1, rk2, e2 = sel
    m = bb * lt
    nl = l // lt
    ne = u.shape[0] // E_TILE
    tok = lambda i, j, kk: i * nl + j
    sub = pl.BlockSpec((P_HEADS, E_SUB, m), lambda i, j, kk: (0, kk, tok(i, j, kk)))
    full = pl.BlockSpec((P_HEADS, P_NKEYS, m), lambda i, j, kk: (0, 0, tok(i, j, kk)))
    xs = pl.BlockSpec((bb, lt, d), lambda i, j, kk: (i, j, 0))
    return pl.pallas_call(
        _peer_dense_kernel,
        grid=(b // bb, nl, ne),
        in_specs=[pl.BlockSpec((m, d), lambda i, j, kk: (tok(i, j, kk), 0)),
                  pl.BlockSpec((E_TILE, d), lambda i, j, kk: (kk, 0)),
                  pl.BlockSpec((d, E_TILE), lambda i, j, kk: (0, kk)),
                  sub, sub, full, full, xs,
                  pl.BlockSpec((bb, 1, d), lambda i, j, kk: (i, 0, 0)),
                  pl.BlockSpec((1, d), lambda i, j, kk: (0, 0))],
        out_specs=xs,
        out_shape=jax.ShapeDtypeStruct((b, l, d), F32),
        scratch_shapes=[pltpu.VMEM((E_TILE, m), F32),
                        pltpu.VMEM((E_TILE, m), BF16),
                        pltpu.VMEM((d, m), F32)],
        compiler_params=_params("parallel", "parallel", "arbitrary"),
        name="peer_dense",
    )(h2, u, v_t, cnt, e1, rk2, e2, x1, ga2, g_post2.reshape(1, d))


def _layer(x, mod, state, prm, bb, lt, chunk, tm_proj):
    b, l, d = x.shape
    t = b * l
    sh1, sc1, ga1, sh2, sc2, ga2 = (mod[:, None, i * d:(i + 1) * d] for i in range(6))

    h = _prenorm(x, sh1, sc1, prm["g_pre1"], bb, lt).reshape(t, d)
    (mparts,) = _multi_matmul(h, [prm["w_m"]], tm_proj, "proj_mlstm")
    q_all, kv0, kv1, kv2 = _multi_matmul(h, [prm["w_q"]] + prm["w_kv"], min(tm_proj, 256), "proj_attn")
    (gates,) = _multi_matmul(h, [prm["w_g"]], tm_proj, "proj_gate")

    mstate = None
    if state is not None:
        mstate = (state[3], state[4], jnp.broadcast_to(state[5][..., None], state[5].shape + (LANE,)),
                  state[6])
    hm, c_new, n_new, m_new, conv_new = _mlstm(
        mparts.reshape(b, l, -1), prm["conv_w"], prm["conv_b"], prm["gate_b"], prm["mh_norm_g"],
        mstate, chunk)

    q_all = q_all.reshape(b, l, -1)
    kvs = [kv.reshape(b, l, 2 * A_WIDTH) for kv in (kv0, kv1, kv2)]
    outs, lses, bufs = [], [], []
    for g in range(N_GROUPS):
        if state is None:
            o, lse = _attn_prompt(q_all, kvs[g], g)
            keep = min(WINDOWS[g], l)
            nbuf = kvs[g][:, l - keep:]
        else:
            wb = state[g].shape[1]
            o, lse, nbuf = _attn_sample(q_all, kvs[g], state[g].reshape(b, wb, 2 * A_WIDTH), g)
        outs.append(o)
        lses.append(lse)
        bufs.append(nbuf.reshape(b, nbuf.shape[1], 2, A_HEADS, A_HEAD_DIM))

    x1, h2 = _merge(hm, outs, lses, gates.reshape(b, l, 2 * d), x, ga1, sh2, sc2,
                    prm["w_a"], prm["w_b"], prm["w_out"], prm["g_post1"], prm["g_pre2"], bb, lt)
    sel = _peer_select(h2, prm["wq_t"], prm["peer_k1"], prm["peer_k2"])
    y = _peer_dense(h2, prm["peer_u"], prm["peer_vt"], sel, x1, ga2, prm["g_post2"], bb, lt)
    return y, (bufs[0], bufs[1], bufs[2], c_new, n_new, m_new[:, :, 0], conv_new)


def kernel(x_prompt, x_sample, cache_win0_kv, cache_win1_kv, cache_win2_kv, state_mlstm_C, state_mlstm_n, state_mlstm_m, state_conv, c_prompt, c_sample, w_ada, b_ada, g_pre1, g_post1, g_pre2, g_post2, w_in, conv_w, conv_b, b_igate, b_fgate, mh_norm_g, w_a, w_b, w_out, peer_wq, peer_k1, peer_k2, peer_u, peer_v):
    depth = w_in.shape[0]
    d = x_prompt.shape[-1]
    nb_p = x_prompt.shape[0]
    yp, ys = x_prompt, x_sample
    new_p, new_s = [], []
    for layer in range(depth):
        w = w_in[layer]
        o_ig = QK_WIDTH + 2 * M_WIDTH
        o_at = o_ig + 2 * M_HEADS
        o_gt = o_at + 3 * N_GROUPS * A_WIDTH
        prm = {
            "g_pre1": g_pre1[layer], "g_post1": g_post1[layer],
            "g_pre2": g_pre2[layer], "g_post2": g_post2[layer],
            "w_m": jnp.concatenate(
                [w[:, :o_at], jnp.zeros((d, GATE_PAD - 2 * M_HEADS), F32)], axis=1).astype(BF16),
            "w_q": jnp.concatenate(
                [w[:, o_at + 3 * A_WIDTH * g:o_at + 3 * A_WIDTH * g + A_WIDTH] for g in range(N_GROUPS)],
                axis=1).astype(BF16),
            "w_kv": [w[:, o_at + 3 * A_WIDTH * g + A_WIDTH:o_at + 3 * A_WIDTH * (g + 1)].astype(BF16)
                     for g in range(N_GROUPS)],
            "w_g": w[:, o_gt:].astype(BF16),
            "conv_w": conv_w[layer], "conv_b": conv_b[layer],
            "gate_b": jnp.concatenate(
                [b_igate[layer], b_fgate[layer], jnp.zeros((GATE_PAD - 2 * M_HEADS,), F32)]).reshape(1, GATE_PAD),
            "mh_norm_g": mh_norm_g[layer],
            "w_a": w_a[layer].astype(BF16), "w_b": w_b[layer].astype(BF16),
            "w_out": w_out[layer].astype(BF16),
            "wq_t": peer_wq[layer].T.astype(BF16),
            "peer_k1": peer_k1[layer], "peer_k2": peer_k2[layer],
            "peer_u": peer_u[layer].astype(BF16),
            "peer_vt": peer_v[layer].T.astype(BF16),
        }
        mod = _adaln(jnp.concatenate([c_prompt, c_sample], axis=0), w_ada[layer], b_ada[layer])
        state = (cache_win0_kv[layer], cache_win1_kv[layer], cache_win2_kv[layer],
                 state_mlstm_C[layer], state_mlstm_n[layer], state_mlstm_m[layer], state_conv[layer])
        lp = yp.shape[1]
        yp, sp = _layer(yp, mod[:nb_p], None, prm, 1, 512, math.gcd(lp, MLSTM_CHUNK), 512)
        bs, ls = ys.shape[0], ys.shape[1]
        ys, ss = _layer(ys, mod[nb_p:], state, prm, bs, ls, ls, bs * ls)
        new_p.append(sp)
        new_s.append(ss)
    stacked_p = tuple(jnp.stack(z) for z in zip(*new_p))
    stacked_s = tuple(jnp.stack(z) for z in zip(*new_s))
    return (yp, ys) + stacked_p + stacked_s
```

```python
import functools
import math

import jax
import jax.numpy as jnp
from jax import lax
from jax.experimental import pallas as pl
from jax.experimental.pallas import tpu as pltpu

F32 = jnp.float32
BF16 = jnp.bfloat16
HIGHEST = lax.Precision.HIGHEST

RMS_EPS = 1e-6
M_HEADS = 4
M_HEAD_DIM = 128
M_WIDTH = M_HEADS * M_HEAD_DIM
CONV_WIDTH = 4
QK_WIDTH = 2 * M_WIDTH
A_HEADS = 4
A_HEAD_DIM = 128
A_WIDTH = A_HEADS * A_HEAD_DIM
WINDOWS = (128, 512, 2048)
DILATIONS = (1, 4, 16)
N_GROUPS = 3
P_HEADS = 8
P_QDIM = 256
P_HALF = P_QDIM // 2
P_NKEYS = 128
P_TOPK = 16

LANE = 128
GATE_PAD = LANE
MLSTM_CHUNK = 256
ATTN_BLOCK = 128
PROMPT_HEADS_PER_STEP = (4, 1, 1)
VMEM_LIMIT = 48 * 1024 * 1024

_NT = (((1,), (1,)), ((), ()))
_TN = (((0,), (0,)), ((), ()))


def _params(*sem):
    return pltpu.CompilerParams(dimension_semantics=sem, vmem_limit_bytes=VMEM_LIMIT)


def _rms(x):
    return x * lax.rsqrt(jnp.mean(x * x, axis=-1, keepdims=True) + RMS_EPS)


def _adaln_kernel(c_ref, w_ref, b_ref, o_ref):
    c = c_ref[...]
    a = (c * jax.nn.sigmoid(c)).astype(BF16)
    o_ref[...] = jnp.dot(a, w_ref[...].astype(BF16), preferred_element_type=F32) + b_ref[...]


def _adaln(c_all, w_ada, b_ada):
    bc, d = c_all.shape
    n = w_ada.shape[1]
    tn = 1024
    return pl.pallas_call(
        _adaln_kernel,
        grid=(n // tn,),
        in_specs=[pl.BlockSpec((bc, d), lambda j: (0, 0)),
                  pl.BlockSpec((d, tn), lambda j: (0, j)),
                  pl.BlockSpec((1, tn), lambda j: (0, j))],
        out_specs=pl.BlockSpec((bc, tn), lambda j: (0, j)),
        out_shape=jax.ShapeDtypeStruct((bc, n), F32),
        compiler_params=_params("parallel"),
        name="adaln",
    )(c_all, w_ada, b_ada.reshape(1, n))


def _prenorm_kernel(x_ref, sh_ref, sc_ref, g_ref, o_ref):
    y = _rms(x_ref[...]) * g_ref[...]
    o_ref[...] = (y * (1.0 + sc_ref[...]) + sh_ref[...]).astype(o_ref.dtype)


def _prenorm(x, shift, scale, g, bb, lt):
    b, l, d = x.shape
    row = pl.BlockSpec((bb, lt, d), lambda i, j: (i, j, 0))
    vec = pl.BlockSpec((bb, 1, d), lambda i, j: (i, 0, 0))
    return pl.pallas_call(
        _prenorm_kernel,
        grid=(b // bb, l // lt),
        in_specs=[row, vec, vec, pl.BlockSpec((1, 1, d), lambda i, j: (0, 0, 0))],
        out_specs=row,
        out_shape=jax.ShapeDtypeStruct((b, l, d), BF16),
        compiler_params=_params("parallel", "parallel"),
        name="prenorm",
    )(x, shift, scale, g.reshape(1, 1, d))


def _mm_kernel(n, a_ref, *refs):
    a = a_ref[...]
    for w_ref, o_ref in zip(refs[:n], refs[n:]):
        o_ref[...] = jnp.dot(a, w_ref[...], preferred_element_type=F32).astype(o_ref.dtype)


def _multi_matmul(a, ws, tm, name):
    m, k = a.shape
    n = len(ws)
    return pl.pallas_call(
        functools.partial(_mm_kernel, n),
        grid=(m // tm,),
        in_specs=[pl.BlockSpec((tm, k), lambda i: (i, 0))]
        + [pl.BlockSpec(w.shape, lambda i: (0, 0)) for w in ws],
        out_specs=[pl.BlockSpec((tm, w.shape[1]), lambda i: (i, 0)) for w in ws],
        out_shape=[jax.ShapeDtypeStruct((m, w.shape[1]), F32) for w in ws],
        compiler_params=_params("parallel"),
        name=name,
    )(a, *ws)


def _log_sigmoid(x):
    return jnp.minimum(x, 0.0) - jnp.log1p(jnp.exp(-jnp.abs(x)))


def _mlstm_kernel(has_state, c, mp_ref, cw_ref, cb_ref, gb_ref, mg_ref, *refs):
    if has_state:
        c0_ref, n0_ref, m0_ref, cv0_ref = refs[:4]
        refs = refs[4:]
    hm_ref, c_ref, n_ref, m_ref, cv_ref, ubuf = refs
    j = pl.program_id(1)
    hd = M_HEAD_DIM
    pad = 8

    @pl.when(j == 0)
    def _init():
        if has_state:
            c_ref[...] = c0_ref[...]
            n_ref[...] = n0_ref[...]
            m_ref[...] = m0_ref[...]
            ubuf[pad - 3:pad, :] = cv0_ref[0]
        else:
            c_ref[...] = jnp.zeros_like(c_ref)
            n_ref[...] = jnp.zeros_like(n_ref)
            m_ref[...] = jnp.zeros_like(m_ref)
            ubuf[pad - 3:pad, :] = jnp.zeros((3, QK_WIDTH), F32)

    ubuf[pad:pad + c, :] = mp_ref[0, :, 0:QK_WIDTH]
    acc = cb_ref[...] + cw_ref[0:1, :] * ubuf[pad - 3:pad - 3 + c, :]
    for t in range(1, CONV_WIDTH):
        acc = acc + cw_ref[t:t + 1, :] * ubuf[pad - 3 + t:pad - 3 + t + c, :]
    tail = ubuf[pad + c - 3:pad + c, :]
    ubuf[pad - 3:pad, :] = tail
    cv_ref[0] = tail
    qk = acc * jax.nn.sigmoid(acc)

    lane = lax.broadcasted_iota(jnp.int32, (c, GATE_PAD), 1)
    graw = mp_ref[0, :, QK_WIDTH + 2 * M_WIDTH:QK_WIDTH + 2 * M_WIDTH + GATE_PAD] + gb_ref[...]
    gates = jnp.where(lane < M_HEADS, graw, jnp.where(lane < 2 * M_HEADS, _log_sigmoid(graw), 0.0))
    row = lax.broadcasted_iota(jnp.int32, (c, c), 0)
    col = lax.broadcasted_iota(jnp.int32, (c, c), 1)
    causal = col <= row
    tril = jnp.where(causal, 1.0, 0.0).astype(F32)
    cum = jnp.dot(tril, gates, precision=HIGHEST, preferred_element_type=F32)

    for h in range(M_HEADS):
        sl = slice(h * hd, (h + 1) * hd)
        qh = qk[:, sl].astype(BF16)
        kh = qk[:, M_WIDTH + h * hd:M_WIDTH + (h + 1) * hd] * (hd ** -0.5)
        vh = mp_ref[0, :, QK_WIDTH + h * hd:QK_WIDTH + (h + 1) * hd].astype(BF16)
        og = mp_ref[0, :, QK_WIDTH + M_WIDTH + h * hd:QK_WIDTH + M_WIDTH + (h + 1) * hd]
        i_col = gates[:, h:h + 1]
        b_col = cum[:, M_HEADS + h:M_HEADS + h + 1]
        lhs = jnp.where(lane == 0, b_col, jnp.where(lane == 1, 1.0, 0.0))
        rhs = jnp.where(lane == 0, 1.0, jnp.where(lane == 1, i_col - b_col, 0.0))
        dmat = lax.dot_general(lhs, rhs, _NT, precision=HIGHEST, preferred_element_type=F32)
        dmat = jnp.where(causal, dmat, -jnp.inf)
        m_prev = m_ref[0, h:h + 1, 0:1]
        inter = b_col + m_prev
        mt = jnp.maximum(inter, jnp.max(dmat, axis=-1, keepdims=True))
        w_in = jnp.exp(dmat - mt)
        w_prev = jnp.exp(inter - mt)
        s = lax.dot_general(qh, kh.astype(BF16), _NT, preferred_element_type=F32) * w_in
        cmat = c_ref[0, h]
        nrow = n_ref[0, h:h + 1, :]
        num = jnp.dot(s.astype(BF16), vh, preferred_element_type=F32) + w_prev * jnp.dot(
            qh, cmat.astype(BF16), preferred_element_type=F32)
        den = jnp.sum(s, axis=-1, keepdims=True) + w_prev * jnp.sum(
            qk[:, sl] * nrow, axis=-1, keepdims=True)
        hh = num / jnp.maximum(jnp.abs(den), jnp.exp(-mt))
        m_new = mt[c - 1:c, :]
        b_last = b_col[c - 1:c, :]
        w_end = jnp.exp(b_last - b_col + i_col - m_new)
        decay = jnp.exp(b_last + m_prev - m_new)
        kw = kh * w_end
        c_ref[0, h] = decay * cmat + lax.dot_general(kw.astype(BF16), vh, _TN, preferred_element_type=F32)
        n_ref[0, h:h + 1, :] = decay * nrow + jnp.sum(kw, axis=0, keepdims=True)
        m_ref[0, h:h + 1, :] = jnp.broadcast_to(m_new, (1, LANE))
        hh = hh * jax.nn.sigmoid(og)
        hm_ref[0, :, sl] = _rms(hh) * mg_ref[:, sl]


def _mlstm(mparts, conv_w, conv_b, gate_b, mh_g, state, c):
    b, l, width = mparts.shape
    has_state = state is not None
    nj = l // c
    per_b4 = lambda i, j: (i, 0, 0, 0)
    per_b3 = lambda i, j: (i, 0, 0)
    const2 = lambda i, j: (0, 0)
    in_specs = [pl.BlockSpec((1, c, width), lambda i, j: (i, j, 0)),
                pl.BlockSpec(conv_w.shape, const2),
                pl.BlockSpec((1, QK_WIDTH), const2),
                pl.BlockSpec((1, GATE_PAD), const2),
                pl.BlockSpec((1, M_WIDTH), const2)]
    args = [mparts, conv_w, conv_b.reshape(1, QK_WIDTH), gate_b, mh_g.reshape(1, M_WIDTH)]
    state_specs = [pl.BlockSpec((1, M_HEADS, M_HEAD_DIM, M_HEAD_DIM), per_b4),
                   pl.BlockSpec((1, M_HEADS, M_HEAD_DIM), per_b3),
                   pl.BlockSpec((1, M_HEADS, LANE), per_b3),
                   pl.BlockSpec((1, CONV_WIDTH - 1, QK_WIDTH), per_b3)]
    if has_state:
        in_specs += state_specs
        args += list(state)
    return pl.pallas_call(
        functools.partial(_mlstm_kernel, has_state, c),
        grid=(b, nj),
        in_specs=in_specs,
        out_specs=[pl.BlockSpec((1, c, M_WIDTH), lambda i, j: (i, j, 0))] + state_specs,
        out_shape=[jax.ShapeDtypeStruct((b, l, M_WIDTH), F32),
                   jax.ShapeDtypeStruct((b, M_HEADS, M_HEAD_DIM, M_HEAD_DIM), F32),
                   jax.ShapeDtypeStruct((b, M_HEADS, M_HEAD_DIM), F32),
                   jax.ShapeDtypeStruct((b, M_HEADS, LANE), F32),
                   jax.ShapeDtypeStruct((b, CONV_WIDTH - 1, QK_WIDTH), F32)],
        scratch_shapes=[pltpu.VMEM((8 + c, QK_WIDTH), F32)],
        compiler_params=_params("parallel", "arbitrary"),
        name="mlstm",
    )(*args)


def _attn_prompt_kernel(dil, hps, use_prev, q_ref, kc_ref, vc_ref, *rest):
    if use_prev:
        kp_ref, vp_ref, o_ref, l_ref = rest
    else:
        o_ref, l_ref = rest
    n = pl.program_id(2)
    blk, hd = ATTN_BLOCK, A_HEAD_DIM
    scale = hd ** -0.5
    row = lax.broadcasted_iota(jnp.int32, (blk, blk), 0)
    col = lax.broadcasted_iota(jnp.int32, (blk, blk), 1)
    cur_ok = col <= row
    prev_ok = (col - row) >= jnp.where(n > 0, 0, blk)

    def residue(r):
        rows = pl.ds(r, blk, stride=dil) if dil > 1 else pl.ds(0, blk)
        for h in range(hps):
            sl = slice(h * hd, (h + 1) * hd)
            q = q_ref[0, rows, sl].astype(BF16)
            s_c = lax.dot_general(q, kc_ref[0, rows, sl].astype(BF16), _NT, preferred_element_type=F32) * scale
            s_c = jnp.where(cur_ok, s_c, -jnp.inf)
            m = jnp.max(s_c, axis=-1, keepdims=True)
            if use_prev:
                s_p = lax.dot_general(q, kp_ref[0, rows, sl].astype(BF16), _NT, preferred_element_type=F32) * scale
                s_p = jnp.where(prev_ok, s_p, -jnp.inf)
                m = jnp.maximum(m, jnp.max(s_p, axis=-1, keepdims=True))
            e_c = jnp.exp(s_c - m)
            den = jnp.sum(e_c, axis=-1, keepdims=True)
            acc = jnp.dot(e_c.astype(BF16), vc_ref[0, rows, sl].astype(BF16), preferred_element_type=F32)
            if use_prev:
                e_p = jnp.exp(s_p - m)
                den = den + jnp.sum(e_p, axis=-1, keepdims=True)
                acc = acc + jnp.dot(e_p.astype(BF16), vp_ref[0, rows, sl].astype(BF16), preferred_element_type=F32)
            o_ref[0, rows, sl] = acc / den
            l_ref[0, rows, sl] = jnp.broadcast_to(m + jnp.log(den), (blk, hd))

    if dil == 1:
        residue(0)
    else:
        def body(r, carry):
            residue(r)
            return carry
        lax.fori_loop(0, dil, body, 0)


def _attn_prompt(q_all, kv, g, hps):
    b, s, _ = q_all.shape
    dil = DILATIONS[g]
    span = dil * ATTN_BLOCK
    assert WINDOWS[g] // dil == ATTN_BLOCK and s % span == 0 and A_HEADS % hps == 0
    nb = s // span
    nh = A_HEADS // hps
    w = hps * A_HEAD_DIM
    use_prev = nb > 1
    cur = lambda off: pl.BlockSpec((1, span, w), lambda i, hs, n: (i, n, off + hs))
    prev = lambda off: pl.BlockSpec((1, span, w), lambda i, hs, n: (i, jnp.maximum(n - 1, 0), off + hs))
    in_specs = [cur(g * nh), cur(0), cur(nh)]
    args = [q_all, kv, kv]
    if use_prev:
        in_specs += [prev(0), prev(nh)]
        args += [kv, kv]
    return pl.pallas_call(
        functools.partial(_attn_prompt_kernel, dil, hps, use_prev),
        grid=(b, nh, nb),
        in_specs=in_specs,
        out_specs=[cur(0), cur(0)],
        out_shape=[jax.ShapeDtypeStruct((b, s, A_WIDTH), F32)] * 2,
        compiler_params=_params("parallel", "parallel", "arbitrary"),
        name="attn_prompt_g%d" % g,
    )(*args)


def _kv_to_cache_kernel(kv_ref, o_ref):
    hd = A_HEAD_DIM
    for j in range(2):
        for h in range(A_HEADS):
            o_ref[0, :, j, h, :] = kv_ref[0, :, j * A_WIDTH + h * hd:j * A_WIDTH + (h + 1) * hd]


def _kv_to_cache(kv, keep):
    b, s, _ = kv.shape
    rt = min(keep, 512)
    assert keep % rt == 0 and (s - keep) % rt == 0
    first = (s - keep) // rt
    return pl.pallas_call(
        _kv_to_cache_kernel,
        grid=(b, keep // rt),
        in_specs=[pl.BlockSpec((1, rt, 2 * A_WIDTH), lambda i, j: (i, first + j, 0))],
        out_specs=pl.BlockSpec((1, rt, 2, A_HEADS, A_HEAD_DIM), lambda i, j: (i, j, 0, 0, 0)),
        out_shape=jax.ShapeDtypeStruct((b, keep, 2, A_HEADS, A_HEAD_DIM), F32),
        compiler_params=_params("parallel", "parallel"),
        name="kv_to_cache",
    )(kv)


def _attn_sample_kernel(window, dil, q_ref, kvn_ref, buf_ref, o_ref, l_ref, nbuf_ref):
    t_new = q_ref.shape[1]
    wb = buf_ref.shape[1]
    hd = A_HEAD_DIM
    scale = hd ** -0.5
    tb = lax.broadcasted_iota(jnp.int32, (t_new, wb), 0)
    eb = lax.broadcasted_iota(jnp.int32, (t_new, wb), 1)
    dist_b = wb + tb - eb
    ok_b = (dist_b <= window) & ((dist_b & (dil - 1)) == 0)
    tn = lax.broadcasted_iota(jnp.int32, (t_new, t_new), 0)
    en = lax.broadcasted_iota(jnp.int32, (t_new, t_new), 1)
    dist_n = tn - en
    ok_n = (dist_n >= 0) & (dist_n <= window) & ((dist_n & (dil - 1)) == 0)
    for h in range(A_HEADS):
        sl = slice(h * hd, (h + 1) * hd)
        sv = slice(A_WIDTH + h * hd, A_WIDTH + (h + 1) * hd)
        q = q_ref[0, :, sl].astype(BF16)
        s_b = lax.dot_general(q, buf_ref[0, :, 0, h, :].astype(BF16), _NT, preferred_element_type=F32) * scale
        s_n = lax.dot_general(q, kvn_ref[0, :, sl].astype(BF16), _NT, preferred_element_type=F32) * scale
        s_b = jnp.where(ok_b, s_b, -jnp.inf)
        s_n = jnp.where(ok_n, s_n, -jnp.inf)
        m = jnp.maximum(jnp.max(s_b, axis=-1, keepdims=True), jnp.max(s_n, axis=-1, keepdims=True))
        e_b = jnp.exp(s_b - m)
        e_n = jnp.exp(s_n - m)
        den = jnp.sum(e_b, axis=-1, keepdims=True) + jnp.sum(e_n, axis=-1, keepdims=True)
        acc = jnp.dot(e_b.astype(BF16), buf_ref[0, :, 1, h, :].astype(BF16), preferred_element_type=F32)
        acc = acc + jnp.dot(e_n.astype(BF16), kvn_ref[0, :, sv].astype(BF16), preferred_element_type=F32)
        o_ref[0, :, sl] = acc / den
        l_ref[0, :, sl] = jnp.broadcast_to(m + jnp.log(den), (t_new, hd))
        nbuf_ref[0, wb - t_new:wb, 0, h, :] = kvn_ref[0, :, sl]
        nbuf_ref[0, wb - t_new:wb, 1, h, :] = kvn_ref[0, :, sv]
    nbuf_ref[0, 0:wb - t_new] = buf_ref[0, t_new:wb]


def _attn_sample(q_all, kvn, buf, g):
    b, t_new, _ = q_all.shape
    wb = buf.shape[1]
    assert wb == WINDOWS[g] and wb >= t_new and t_new % 8 == 0
    assert DILATIONS[g] & (DILATIONS[g] - 1) == 0
    row = lambda i: (i, 0, 0)
    o_spec = pl.BlockSpec((1, t_new, A_WIDTH), row)
    c_spec = pl.BlockSpec((1, wb, 2, A_HEADS, A_HEAD_DIM), lambda i: (i, 0, 0, 0, 0))
    return pl.pallas_call(
        functools.partial(_attn_sample_kernel, WINDOWS[g], DILATIONS[g]),
        grid=(b,),
        in_specs=[pl.BlockSpec((1, t_new, A_WIDTH), lambda i: (i, 0, g)),
                  pl.BlockSpec((1, t_new, 2 * A_WIDTH), row),
                  c_spec],
        out_specs=[o_spec, o_spec, c_spec],
        out_shape=[jax.ShapeDtypeStruct((b, t_new, A_WIDTH), F32),
                   jax.ShapeDtypeStruct((b, t_new, A_WIDTH), F32),
                   jax.ShapeDtypeStruct(buf.shape, F32)],
        compiler_params=_params("parallel"),
        name="attn_sample_g%d" % g,
    )(q_all, kvn, buf)


def _merge_kernel(hm_ref, o0_ref, o1_ref, o2_ref, l0_ref, l1_ref, l2_ref, gate_ref, x_ref,
                  ga1_ref, sh2_ref, sc2_ref, wa_ref, wb_ref, wo_ref, gp1_ref, gp2_ref,
                  x1_ref, h2_ref):
    bb, lt, d = x_ref.shape
    rows = bb * lt
    two = lambda ref: ref[...].reshape(rows, ref.shape[-1])
    l0, l1, l2 = two(l0_ref), two(l1_ref), two(l2_ref)
    m = jnp.maximum(jnp.maximum(l0, l1), l2)
    e0, e1, e2 = jnp.exp(l0 - m), jnp.exp(l1 - m), jnp.exp(l2 - m)
    oa = (e0 * two(o0_ref) + e1 * two(o1_ref) + e2 * two(o2_ref)) / (e0 + e1 + e2)
    br_a = jnp.dot(two(hm_ref).astype(BF16), wa_ref[...], preferred_element_type=F32)
    br_b = jnp.dot(oa.astype(BF16), wb_ref[...], preferred_element_type=F32)
    gate = two(gate_ref)
    merged = jax.nn.sigmoid(gate[:, 0:d]) * br_a + jax.nn.sigmoid(gate[:, d:2 * d]) * br_b
    y = jnp.dot(merged.astype(BF16), wo_ref[...], preferred_element_type=F32)
    y = (_rms(y) * gp1_ref[...]).reshape(bb, lt, d)
    x1 = x_ref[...] + ga1_ref[...] * y
    x1_ref[...] = x1
    h2 = (_rms(x1) * gp2_ref[...].reshape(1, 1, d)) * (1.0 + sc2_ref[...]) + sh2_ref[...]
    h2_ref[...] = h2.reshape(rows, d).astype(h2_ref.dtype)


def _merge(hm, outs, lses, gates, x, ga1, sh2, sc2, w_a, w_b, w_out, g_post1, g_pre2, bb, lt):
    b, l, d = x.shape
    rows = bb * lt
    nl = l // lt
    idx = lambda i, j: (i, j, 0)
    vec = pl.BlockSpec((bb, 1, d), lambda i, j: (i, 0, 0))
    const = lambda i, j: (0, 0)
    aw = pl.BlockSpec((bb, lt, A_WIDTH), idx)
    return pl.pallas_call(
        _merge_kernel,
        grid=(b // bb, nl),
        in_specs=[pl.BlockSpec((bb, lt, M_WIDTH), idx), aw, aw, aw, aw, aw, aw,
                  pl.BlockSpec((bb, lt, 2 * d), idx), pl.BlockSpec((bb, lt, d), idx),
                  vec, vec, vec,
                  pl.BlockSpec(w_a.shape, const), pl.BlockSpec(w_b.shape, const),
                  pl.BlockSpec(w_out.shape, const),
                  pl.BlockSpec((1, d), const), pl.BlockSpec((1, d), const)],
        out_specs=[pl.BlockSpec((bb, lt, d), idx),
                   pl.BlockSpec((rows, d), lambda i, j: (i * nl + j, 0))],
        out_shape=[jax.ShapeDtypeStruct((b, l, d), F32),
                   jax.ShapeDtypeStruct((b * l, d), BF16)],
        compiler_params=_params("parallel", "parallel"),
        name="merge",
    )(hm, *outs, *lses, gates, x, ga1, sh2, sc2, w_a, w_b, w_out,
      g_post1.reshape(1, d), g_pre2.reshape(1, d))


def _topk_rows(x, k):
    n, m = x.shape
    ridx = lax.broadcasted_iota(jnp.int32, (n, m), 0).astype(F32)
    kidx = lax.broadcasted_iota(jnp.int32, (k, m), 0)

    def body(r, carry):
        x, rank, vals = carry
        mx = jnp.max(x, axis=0, keepdims=True)
        first = jnp.min(jnp.where(x == mx, ridx, float(n)), axis=0, keepdims=True)
        sel = ridx == first
        rank = jnp.where(sel, r.astype(F32), rank)
        x = jnp.where(sel, -jnp.inf, x)
        vals = jnp.where(kidx == r, mx, vals)
        return x, rank, vals

    _, rank, vals = lax.fori_loop(
        0, k, body, (x, jnp.full((n, m), float(k), F32), jnp.zeros((k, m), F32)))
    return vals, rank


def _peer_select_kernel(h2_ref, wqt_ref, k1_ref, k2_ref, cnt_ref, e1_ref, rk2_ref, e2_ref, qt_s):
    m = h2_ref.shape[0]
    k = P_TOPK
    qt_s[...] = lax.dot_general(wqt_ref[...], h2_ref[...], _NT, preferred_element_type=F32)
    k1 = k1_ref[...].astype(BF16)
    k2 = k2_ref[...].astype(BF16)
    i16 = lax.broadcasted_iota(jnp.int32, (k, m), 0)
    i8 = lax.broadcasted_iota(jnp.int32, (8, m), 0)
    pos = [(i16 * k).astype(F32), (i8 * k + 1).astype(F32)]
    pos += [(i8 * k + jj).astype(F32) for jj in range(2, 8)]
    pos += [(i8 + 8).astype(F32)]
    pos = jnp.concatenate(pos, axis=0)
    ncand = pos.shape[0]

    def per_head(h, _):
        qa = qt_s[pl.ds(pl.multiple_of(h * P_QDIM, P_QDIM), P_HALF), :].astype(BF16)
        qb = qt_s[pl.ds(pl.multiple_of(h * P_QDIM + P_HALF, P_HALF), P_HALF), :].astype(BF16)
        s1 = jnp.dot(k1, qa, preferred_element_type=F32)
        s2 = jnp.dot(k2, qb, preferred_element_type=F32)
        v1, rk1 = _topk_rows(s1, k)
        v2, rk2 = _topk_rows(s2, k)
        blocks = [v1 + v2[0:1], v1[0:8] + v2[1:2]]
        for jj in range(2, 8):
            blocks.append(jnp.where(i8 < k // (jj + 1), v1[0:8] + v2[jj:jj + 1], -jnp.inf))
        blocks.append(v1[0:1] + v2[8:16])
        cand = jnp.concatenate(blocks, axis=0)
        cmax = v1[0:1] + v2[0:1]

        def pick(r, carry):
            cand, selm, z = carry
            mx = jnp.max(cand, axis=0, keepdims=True)
            first = jnp.min(jnp.where(cand == mx, pos, 1e9), axis=0, keepdims=True)
            sel = pos == first
            selm = jnp.where(sel, 1.0, selm)
            cand = jnp.where(sel, -jnp.inf, cand)
            return cand, selm, z + jnp.exp(mx - cmax)

        _, selm, z = lax.fori_loop(
            0, k, pick, (cand, jnp.zeros((ncand, m), F32), jnp.zeros((1, m), F32)))
        low = selm[16:24]
        for jj in range(2, 8):
            low = low + selm[8 + 8 * jj:16 + 8 * jj]
        cnt16 = selm[0:16] + jnp.concatenate([low, jnp.zeros((8, m), F32)], axis=0)
        cnt16 = cnt16 + jnp.where(i16 == 0, jnp.sum(selm[72:80], axis=0, keepdims=True), 0.0)
        cnt = jnp.zeros_like(s1)
        for r in range(k):
            cnt = cnt + jnp.where(rk1 == float(r), cnt16[r:r + 1], 0.0)
        cnt_ref[h] = cnt
        e1_ref[h] = jnp.exp(s1 - v1[0:1]) / z
        rk2_ref[h] = rk2
        e2_ref[h] = jnp.exp(s2 - v2[0:1])
        return 0

    lax.fori_loop(0, P_HEADS, per_head, 0)


def _peer_select(h2, wq_t, k1, k2):
    t, d = h2.shape
    tm = LANE
    sel_spec = pl.BlockSpec((P_HEADS, P_NKEYS, tm), lambda i: (0, 0, i))
    sel_shape = jax.ShapeDtypeStruct((P_HEADS, P_NKEYS, t), F32)
    return pl.pallas_call(
        _peer_select_kernel,
        grid=(t // tm,),
        in_specs=[pl.BlockSpec((tm, d), lambda i: (i, 0)),
                  pl.BlockSpec(wq_t.shape, lambda i: (0, 0)),
                  pl.BlockSpec(k1.shape, lambda i: (0, 0)),
                  pl.BlockSpec(k2.shape, lambda i: (0, 0))],
        out_specs=[sel_spec] * 4,
        out_shape=[sel_shape] * 4,
        scratch_shapes=[pltpu.VMEM((P_HEADS * P_QDIM, tm), F32)],
        compiler_params=_params("parallel"),
        name="peer_select",
    )(h2, wq_t, k1, k2)


E_SUB = 8
E_TILE = E_SUB * P_NKEYS


def _gelu(x):
    return 0.5 * x * (1.0 + lax.erf(x * math.sqrt(0.5)))


def _peer_dense_kernel(h2_ref, u_ref, vt_ref, cnt_ref, e1_ref, rk2_ref, e2_ref,
                       x1_ref, ga2_ref, gp2_ref, o_ref, s_s, a_s, acc_s):
    kk = pl.program_id(2)
    bb, lt, d = x1_ref.shape
    m = bb * lt

    @pl.when(kk == 0)
    def _zero():
        acc_s[...] = jnp.zeros_like(acc_s)

    s_s[...] = lax.dot_general(u_ref[...], h2_ref[...], _NT, preferred_element_type=F32)
    for i in range(E_SUB):
        rows = slice(i * P_NKEYS, (i + 1) * P_NKEYS)
        w = jnp.zeros((P_NKEYS, m), F32)
        for h in range(P_HEADS):
            picked = rk2_ref[h] < cnt_ref[h, i:i + 1, :]
            w = w + jnp.where(picked, e2_ref[h], 0.0) * e1_ref[h, i:i + 1, :]
        a_s[rows, :] = (w * _gelu(s_s[rows, :])).astype(BF16)
    acc_s[...] += jnp.dot(vt_ref[...], a_s[...], preferred_element_type=F32)

    @pl.when(kk == pl.num_programs(2) - 1)
    def _finish():
        f = acc_s[...].T
        y = (_rms(f) * gp2_ref[...]).reshape(bb, lt, d)
        o_ref[...] = x1_ref[...] + ga2_ref[...] * y


def _peer_dense(h2, u, v_t, sel, x1, ga2, g_post2, bb, lt):
    b, l, d = x1.shape
    cnt, e1, rk2, e2 = sel
    m = bb * lt
    nl = l // lt
    ne = u.shape[0] // E_TILE
    tok = lambda i, j, kk: i * nl + j
    sub = pl.BlockSpec((P_HEADS, E_SUB, m), lambda i, j, kk: (0, kk, tok(i, j, kk)))
    full = pl.BlockSpec((P_HEADS, P_NKEYS, m), lambda i, j, kk: (0, 0, tok(i, j, kk)))
    xs = pl.BlockSpec((bb, lt, d), lambda i, j, kk: (i, j, 0))
    return pl.pallas_call(
        _peer_dense_kernel,
        grid=(b // bb, nl, ne),
        in_specs=[pl.BlockSpec((m, d), lambda i, j, kk: (tok(i, j, kk), 0)),
                  pl.BlockSpec((E_TILE, d), lambda i, j, kk: (kk, 0)),
                  pl.BlockSpec((d, E_TILE), lambda i, j, kk: (0, kk)),
                  sub, sub, full, full, xs,
                  pl.BlockSpec((bb, 1, d), lambda i, j, kk: (i, 0, 0)),
                  pl.BlockSpec((1, d), lambda i, j, kk: (0, 0))],
        out_specs=xs,
        out_shape=jax.ShapeDtypeStruct((b, l, d), F32),
        scratch_shapes=[pltpu.VMEM((E_TILE, m), F32),
                        pltpu.VMEM((E_TILE, m), BF16),
                        pltpu.VMEM((d, m), F32)],
        compiler_params=_params("parallel", "parallel", "arbitrary"),
        name="peer_dense",
    )(h2, u, v_t, cnt, e1, rk2, e2, x1, ga2, g_post2.reshape(1, d))


def _layer(x, mod, state, prm, bb, lt, chunk, tm_proj):
    b, l, d = x.shape
    t = b * l
    sh1, sc1, ga1, sh2, sc2, ga2 = (mod[:, None, i * d:(i + 1) * d] for i in range(6))

    h = _prenorm(x, sh1, sc1, prm["g_pre1"], bb, lt).reshape(t, d)
    (mparts,) = _multi_matmul(h, [prm["w_m"]], tm_proj, "proj_mlstm")
    q_all, kv0, kv1, kv2 = _multi_matmul(h, [prm["w_q"]] + prm["w_kv"], min(tm_proj, 256), "proj_attn")
    (gates,) = _multi_matmul(h, [prm["w_g"]], tm_proj, "proj_gate")

    mstate = None
    if state is not None:
        mstate = (state[3], state[4], jnp.broadcast_to(state[5][..., None], state[5].shape + (LANE,)),
                  state[6])
    hm, c_new, n_new, m_new, conv_new = _mlstm(
        mparts.reshape(b, l, -1), prm["conv_w"], prm["conv_b"], prm["gate_b"], prm["mh_norm_g"],
        mstate, chunk)

    q_all = q_all.reshape(b, l, -1)
    kvs = [kv.reshape(b, l, 2 * A_WIDTH) for kv in (kv0, kv1, kv2)]
    outs, lses, bufs = [], [], []
    for g in range(N_GROUPS):
        if state is None:
            o, lse = _attn_prompt(q_all, kvs[g], g, PROMPT_HEADS_PER_STEP[g])
            nbuf = _kv_to_cache(kvs[g], min(WINDOWS[g], l))
        else:
            o, lse, nbuf = _attn_sample(q_all, kvs[g], state[g], g)
        outs.append(o)
        lses.append(lse)
        bufs.append(nbuf)

    x1, h2 = _merge(hm, outs, lses, gates.reshape(b, l, 2 * d), x, ga1, sh2, sc2,
                    prm["w_a"], prm["w_b"], prm["w_out"], prm["g_post1"], prm["g_pre2"], bb, lt)
    sel = _peer_select(h2, prm["wq_t"], prm["peer_k1"], prm["peer_k2"])
    y = _peer_dense(h2, prm["peer_u"], prm["peer_vt"], sel, x1, ga2, prm["g_post2"], bb, lt)
    return y, (bufs[0], bufs[1], bufs[2], c_new, n_new, m_new[:, :, 0], conv_new)


def kernel(x_prompt, x_sample, cache_win0_kv, cache_win1_kv, cache_win2_kv, state_mlstm_C, state_mlstm_n, state_mlstm_m, state_conv, c_prompt, c_sample, w_ada, b_ada, g_pre1, g_post1, g_pre2, g_post2, w_in, conv_w, conv_b, b_igate, b_fgate, mh_norm_g, w_a, w_b, w_out, peer_wq, peer_k1, peer_k2, peer_u, peer_v):
    depth = w_in.shape[0]
    d = x_prompt.shape[-1]
    nb_p = x_prompt.shape[0]
    yp, ys = x_prompt, x_sample
    new_p, new_s = [], []
    for layer in range(depth):
        w = w_in[layer]
        o_ig = QK_WIDTH + 2 * M_WIDTH
        o_at = o_ig + 2 * M_HEADS
        o_gt = o_at + 3 * N_GROUPS * A_WIDTH
        prm = {
            "g_pre1": g_pre1[layer], "g_post1": g_post1[layer],
            "g_pre2": g_pre2[layer], "g_post2": g_post2[layer],
            "w_m": jnp.concatenate(
                [w[:, :o_at], jnp.zeros((d, GATE_PAD - 2 * M_HEADS), F32)], axis=1).astype(BF16),
            "w_q": jnp.concatenate(
                [w[:, o_at + 3 * A_WIDTH * g:o_at + 3 * A_WIDTH * g + A_WIDTH] for g in range(N_GROUPS)],
                axis=1).astype(BF16),
            "w_kv": [w[:, o_at + 3 * A_WIDTH * g + A_WIDTH:o_at + 3 * A_WIDTH * (g + 1)].astype(BF16)
                     for g in range(N_GROUPS)],
            "w_g": w[:, o_gt:].astype(BF16),
            "conv_w": conv_w[layer], "conv_b": conv_b[layer],
            "gate_b": jnp.concatenate(
                [b_igate[layer], b_fgate[layer], jnp.zeros((GATE_PAD - 2 * M_HEADS,), F32)]).reshape(1, GATE_PAD),
            "mh_norm_g": mh_norm_g[layer],
            "w_a": w_a[layer].astype(BF16), "w_b": w_b[layer].astype(BF16),
            "w_out": w_out[layer].astype(BF16),
            "wq_t": peer_wq[layer].T.astype(BF16),
            "peer_k1": peer_k1[layer], "peer_k2": peer_k2[layer],
            "peer_u": peer_u[layer].astype(BF16),
            "peer_vt": peer_v[layer].T.astype(BF16),
        }
        mod = _adaln(jnp.concatenate([c_prompt, c_sample], axis=0), w_ada[layer], b_ada[layer])
        state = (cache_win0_kv[layer], cache_win1_kv[layer], cache_win2_kv[layer],
                 state_mlstm_C[layer], state_mlstm_n[layer], state_mlstm_m[layer], state_conv[layer])
        lp = yp.shape[1]
        yp, sp = _layer(yp, mod[:nb_p], None, prm, 1, 512, math.gcd(lp, MLSTM_CHUNK), 512)
        bs, ls = ys.shape[0], ys.shape[1]
        ys, ss = _layer(ys, mod[nb_p:], state, prm, bs, ls, ls, bs * ls)
        new_p.append(sp)
        new_s.append(ss)
    stacked_p = tuple(jnp.stack(z) for z in zip(*new_p))
    stacked_s = tuple(jnp.stack(z) for z in zip(*new_s))
    return (yp, ys) + stacked_p + stacked_s
```

```python
import functools
import math

import jax
import jax.numpy as jnp
from jax import lax
from jax.experimental import pallas as pl
from jax.experimental.pallas import tpu as pltpu

F32 = jnp.float32
BF16 = jnp.bfloat16
HIGHEST = lax.Precision.HIGHEST

RMS_EPS = 1e-6
M_HEADS = 4
M_HEAD_DIM = 128
M_WIDTH = M_HEADS * M_HEAD_DIM
CONV_WIDTH = 4
QK_WIDTH = 2 * M_WIDTH
A_HEADS = 4
A_HEAD_DIM = 128
A_WIDTH = A_HEADS * A_HEAD_DIM
WINDOWS = (128, 512, 2048)
DILATIONS = (1, 4, 16)
N_GROUPS = 3
P_HEADS = 8
P_QDIM = 256
P_HALF = P_QDIM // 2
P_NKEYS = 128
P_TOPK = 16

LANE = 128
GATE_PAD = LANE
MLSTM_CHUNK = 256
ATTN_BLOCK = 128
SELECT_TOKENS = 2 * LANE
PROMPT_HEADS_PER_STEP = (4, 1, 1)
VMEM_LIMIT = 48 * 1024 * 1024

_NT = (((1,), (1,)), ((), ()))
_TN = (((0,), (0,)), ((), ()))


def _params(*sem):
    return pltpu.CompilerParams(dimension_semantics=sem, vmem_limit_bytes=VMEM_LIMIT)


def _rms(x):
    return x * lax.rsqrt(jnp.mean(x * x, axis=-1, keepdims=True) + RMS_EPS)


def _adaln_kernel(c_ref, w_ref, b_ref, o_ref):
    c = c_ref[...]
    a = (c * jax.nn.sigmoid(c)).astype(BF16)
    o_ref[...] = jnp.dot(a, w_ref[...].astype(BF16), preferred_element_type=F32) + b_ref[...]


def _adaln(c_all, w_ada, b_ada):
    bc, d = c_all.shape
    n = w_ada.shape[1]
    tn = 1024
    return pl.pallas_call(
        _adaln_kernel,
        grid=(n // tn,),
        in_specs=[pl.BlockSpec((bc, d), lambda j: (0, 0)),
                  pl.BlockSpec((d, tn), lambda j: (0, j)),
                  pl.BlockSpec((1, tn), lambda j: (0, j))],
        out_specs=pl.BlockSpec((bc, tn), lambda j: (0, j)),
        out_shape=jax.ShapeDtypeStruct((bc, n), F32),
        compiler_params=_params("parallel"),
        name="adaln",
    )(c_all, w_ada, b_ada.reshape(1, n))


def _prenorm_kernel(x_ref, sh_ref, sc_ref, g_ref, o_ref):
    y = _rms(x_ref[...]) * g_ref[...]
    o_ref[...] = (y * (1.0 + sc_ref[...]) + sh_ref[...]).astype(o_ref.dtype)


def _prenorm(x, shift, scale, g, bb, lt):
    b, l, d = x.shape
    row = pl.BlockSpec((bb, lt, d), lambda i, j: (i, j, 0))
    vec = pl.BlockSpec((bb, 1, d), lambda i, j: (i, 0, 0))
    return pl.pallas_call(
        _prenorm_kernel,
        grid=(b // bb, l // lt),
        in_specs=[row, vec, vec, pl.BlockSpec((1, 1, d), lambda i, j: (0, 0, 0))],
        out_specs=row,
        out_shape=jax.ShapeDtypeStruct((b, l, d), BF16),
        compiler_params=_params("parallel", "parallel"),
        name="prenorm",
    )(x, shift, scale, g.reshape(1, 1, d))


def _mm_kernel(n, a_ref, *refs):
    a = a_ref[...]
    for w_ref, o_ref in zip(refs[:n], refs[n:]):
        o_ref[...] = jnp.dot(a, w_ref[...], preferred_element_type=F32).astype(o_ref.dtype)


def _multi_matmul(a, ws, tm, name):
    m, k = a.shape
    n = len(ws)
    return pl.pallas_call(
        functools.partial(_mm_kernel, n),
        grid=(m // tm,),
        in_specs=[pl.BlockSpec((tm, k), lambda i: (i, 0))]
        + [pl.BlockSpec(w.shape, lambda i: (0, 0)) for w in ws],
        out_specs=[pl.BlockSpec((tm, w.shape[1]), lambda i: (i, 0)) for w in ws],
        out_shape=[jax.ShapeDtypeStruct((m, w.shape[1]), F32) for w in ws],
        compiler_params=_params("parallel"),
        name=name,
    )(a, *ws)


def _log_sigmoid(x):
    return jnp.minimum(x, 0.0) - jnp.log1p(jnp.exp(-jnp.abs(x)))


def _mlstm_kernel(has_state, c, mp_ref, cw_ref, cb_ref, gb_ref, mg_ref, *refs):
    if has_state:
        c0_ref, n0_ref, m0_ref, cv0_ref = refs[:4]
        refs = refs[4:]
    hm_ref, c_ref, n_ref, m_ref, cv_ref, ubuf = refs
    j = pl.program_id(1)
    hd = M_HEAD_DIM
    pad = 8

    @pl.when(j == 0)
    def _init():
        if has_state:
            c_ref[...] = c0_ref[...]
            n_ref[...] = n0_ref[...]
            m_ref[...] = m0_ref[...]
            ubuf[pad - 3:pad, :] = cv0_ref[0]
        else:
            c_ref[...] = jnp.zeros_like(c_ref)
            n_ref[...] = jnp.zeros_like(n_ref)
            m_ref[...] = jnp.zeros_like(m_ref)
            ubuf[pad - 3:pad, :] = jnp.zeros((3, QK_WIDTH), F32)

    ubuf[pad:pad + c, :] = mp_ref[0, :, 0:QK_WIDTH]
    acc = cb_ref[...] + cw_ref[0:1, :] * ubuf[pad - 3:pad - 3 + c, :]
    for t in range(1, CONV_WIDTH):
        acc = acc + cw_ref[t:t + 1, :] * ubuf[pad - 3 + t:pad - 3 + t + c, :]
    tail = ubuf[pad + c - 3:pad + c, :]
    ubuf[pad - 3:pad, :] = tail
    cv_ref[0] = tail
    qk = acc * jax.nn.sigmoid(acc)

    lane = lax.broadcasted_iota(jnp.int32, (c, GATE_PAD), 1)
    graw = mp_ref[0, :, QK_WIDTH + 2 * M_WIDTH:QK_WIDTH + 2 * M_WIDTH + GATE_PAD] + gb_ref[...]
    gates = jnp.where(lane < M_HEADS, graw, jnp.where(lane < 2 * M_HEADS, _log_sigmoid(graw), 0.0))
    row = lax.broadcasted_iota(jnp.int32, (c, c), 0)
    col = lax.broadcasted_iota(jnp.int32, (c, c), 1)
    causal = col <= row
    tril = jnp.where(causal, 1.0, 0.0).astype(F32)
    cum = jnp.dot(tril, gates, precision=HIGHEST, preferred_element_type=F32)

    for h in range(M_HEADS):
        sl = slice(h * hd, (h + 1) * hd)
        qh = qk[:, sl].astype(BF16)
        kh = qk[:, M_WIDTH + h * hd:M_WIDTH + (h + 1) * hd] * (hd ** -0.5)
        vh = mp_ref[0, :, QK_WIDTH + h * hd:QK_WIDTH + (h + 1) * hd].astype(BF16)
        og = mp_ref[0, :, QK_WIDTH + M_WIDTH + h * hd:QK_WIDTH + M_WIDTH + (h + 1) * hd]
        i_col = gates[:, h:h + 1]
        b_col = cum[:, M_HEADS + h:M_HEADS + h + 1]
        lhs = jnp.where(lane == 0, b_col, jnp.where(lane == 1, 1.0, 0.0))
        rhs = jnp.where(lane == 0, 1.0, jnp.where(lane == 1, i_col - b_col, 0.0))
        dmat = lax.dot_general(lhs, rhs, _NT, precision=HIGHEST, preferred_element_type=F32)
        dmat = jnp.where(causal, dmat, -jnp.inf)
        m_prev = m_ref[0, h:h + 1, 0:1]
        inter = b_col + m_prev
        mt = jnp.maximum(inter, jnp.max(dmat, axis=-1, keepdims=True))
        w_in = jnp.exp(dmat - mt)
        w_prev = jnp.exp(inter - mt)
        s = lax.dot_general(qh, kh.astype(BF16), _NT, preferred_element_type=F32) * w_in
        cmat = c_ref[0, h]
        nrow = n_ref[0, h:h + 1, :]
        num = jnp.dot(s.astype(BF16), vh, preferred_element_type=F32) + w_prev * jnp.dot(
            qh, cmat.astype(BF16), preferred_element_type=F32)
        den = jnp.sum(s, axis=-1, keepdims=True) + w_prev * jnp.sum(
            qk[:, sl] * nrow, axis=-1, keepdims=True)
        hh = num / jnp.maximum(jnp.abs(den), jnp.exp(-mt))
        m_new = mt[c - 1:c, :]
        b_last = b_col[c - 1:c, :]
        w_end = jnp.exp(b_last - b_col + i_col - m_new)
        decay = jnp.exp(b_last + m_prev - m_new)
        kw = kh * w_end
        c_ref[0, h] = decay * cmat + lax.dot_general(kw.astype(BF16), vh, _TN, preferred_element_type=F32)
        n_ref[0, h:h + 1, :] = decay * nrow + jnp.sum(kw, axis=0, keepdims=True)
        m_ref[0, h:h + 1, :] = jnp.broadcast_to(m_new, (1, LANE))
        hh = hh * jax.nn.sigmoid(og)
        hm_ref[0, :, sl] = _rms(hh) * mg_ref[:, sl]


def _mlstm(mparts, conv_w, conv_b, gate_b, mh_g, state, c):
    b, l, width = mparts.shape
    has_state = state is not None
    nj = l // c
    per_b4 = lambda i, j: (i, 0, 0, 0)
    per_b3 = lambda i, j: (i, 0, 0)
    const2 = lambda i, j: (0, 0)
    in_specs = [pl.BlockSpec((1, c, width), lambda i, j: (i, j, 0)),
                pl.BlockSpec(conv_w.shape, const2),
                pl.BlockSpec((1, QK_WIDTH), const2),
                pl.BlockSpec((1, GATE_PAD), const2),
                pl.BlockSpec((1, M_WIDTH), const2)]
    args = [mparts, conv_w, conv_b.reshape(1, QK_WIDTH), gate_b, mh_g.reshape(1, M_WIDTH)]
    state_specs = [pl.BlockSpec((1, M_HEADS, M_HEAD_DIM, M_HEAD_DIM), per_b4),
                   pl.BlockSpec((1, M_HEADS, M_HEAD_DIM), per_b3),
                   pl.BlockSpec((1, M_HEADS, LANE), per_b3),
                   pl.BlockSpec((1, CONV_WIDTH - 1, QK_WIDTH), per_b3)]
    if has_state:
        in_specs += state_specs
        args += list(state)
    return pl.pallas_call(
        functools.partial(_mlstm_kernel, has_state, c),
        grid=(b, nj),
        in_specs=in_specs,
        out_specs=[pl.BlockSpec((1, c, M_WIDTH), lambda i, j: (i, j, 0))] + state_specs,
        out_shape=[jax.ShapeDtypeStruct((b, l, M_WIDTH), F32),
                   jax.ShapeDtypeStruct((b, M_HEADS, M_HEAD_DIM, M_HEAD_DIM), F32),
                   jax.ShapeDtypeStruct((b, M_HEADS, M_HEAD_DIM), F32),
                   jax.ShapeDtypeStruct((b, M_HEADS, LANE), F32),
                   jax.ShapeDtypeStruct((b, CONV_WIDTH - 1, QK_WIDTH), F32)],
        scratch_shapes=[pltpu.VMEM((8 + c, QK_WIDTH), F32)],
        compiler_params=_params("parallel", "arbitrary"),
        name="mlstm",
    )(*args)


def _attn_prompt_kernel(dil, hps, use_prev, q_ref, kc_ref, vc_ref, *rest):
    if use_prev:
        kp_ref, vp_ref, o_ref, l_ref = rest
    else:
        o_ref, l_ref = rest
    n = pl.program_id(2)
    blk, hd = ATTN_BLOCK, A_HEAD_DIM
    scale = hd ** -0.5
    row = lax.broadcasted_iota(jnp.int32, (blk, blk), 0)
    col = lax.broadcasted_iota(jnp.int32, (blk, blk), 1)
    cur_ok = col <= row
    prev_ok = (col - row) >= jnp.where(n > 0, 0, blk)

    def residue(r):
        rows = pl.ds(r, blk, stride=dil) if dil > 1 else pl.ds(0, blk)
        for h in range(hps):
            sl = slice(h * hd, (h + 1) * hd)
            q = q_ref[0, rows, sl].astype(BF16)
            s_c = lax.dot_general(q, kc_ref[0, rows, sl].astype(BF16), _NT, preferred_element_type=F32) * scale
            s_c = jnp.where(cur_ok, s_c, -jnp.inf)
            m = jnp.max(s_c, axis=-1, keepdims=True)
            if use_prev:
                s_p = lax.dot_general(q, kp_ref[0, rows, sl].astype(BF16), _NT, preferred_element_type=F32) * scale
                s_p = jnp.where(prev_ok, s_p, -jnp.inf)
                m = jnp.maximum(m, jnp.max(s_p, axis=-1, keepdims=True))
            e_c = jnp.exp(s_c - m)
            den = jnp.sum(e_c, axis=-1, keepdims=True)
            acc = jnp.dot(e_c.astype(BF16), vc_ref[0, rows, sl].astype(BF16), preferred_element_type=F32)
            if use_prev:
                e_p = jnp.exp(s_p - m)
                den = den + jnp.sum(e_p, axis=-1, keepdims=True)
                acc = acc + jnp.dot(e_p.astype(BF16), vp_ref[0, rows, sl].astype(BF16), preferred_element_type=F32)
            o_ref[0, rows, sl] = acc / den
            l_ref[0, rows, sl] = jnp.broadcast_to(m + jnp.log(den), (blk, hd))

    if dil == 1:
        residue(0)
    else:
        def body(r, carry):
            residue(r)
            return carry
        lax.fori_loop(0, dil, body, 0)


def _attn_prompt(q_all, kv, g, hps):
    b, s, _ = q_all.shape
    dil = DILATIONS[g]
    span = dil * ATTN_BLOCK
    assert WINDOWS[g] // dil == ATTN_BLOCK and s % span == 0 and A_HEADS % hps == 0
    nb = s // span
    nh = A_HEADS // hps
    w = hps * A_HEAD_DIM
    use_prev = nb > 1
    cur = lambda off: pl.BlockSpec((1, span, w), lambda i, hs, n: (i, n, off + hs))
    prev = lambda off: pl.BlockSpec((1, span, w), lambda i, hs, n: (i, jnp.maximum(n - 1, 0), off + hs))
    in_specs = [cur(g * nh), cur(0), cur(nh)]
    args = [q_all, kv, kv]
    if use_prev:
        in_specs += [prev(0), prev(nh)]
        args += [kv, kv]
    return pl.pallas_call(
        functools.partial(_attn_prompt_kernel, dil, hps, use_prev),
        grid=(b, nh, nb),
        in_specs=in_specs,
        out_specs=[cur(0), cur(0)],
        out_shape=[jax.ShapeDtypeStruct((b, s, A_WIDTH), F32)] * 2,
        compiler_params=_params("parallel", "parallel", "arbitrary"),
        name="attn_prompt_g%d" % g,
    )(*args)


def _kv_to_cache_kernel(kv_ref, o_ref):
    hd = A_HEAD_DIM
    for j in range(2):
        for h in range(A_HEADS):
            o_ref[0, :, j, h, :] = kv_ref[0, :, j * A_WIDTH + h * hd:j * A_WIDTH + (h + 1) * hd]


def _kv_to_cache(kv, keep):
    b, s, _ = kv.shape
    rt = min(keep, 512)
    assert keep % rt == 0 and (s - keep) % rt == 0
    first = (s - keep) // rt
    return pl.pallas_call(
        _kv_to_cache_kernel,
        grid=(b, keep // rt),
        in_specs=[pl.BlockSpec((1, rt, 2 * A_WIDTH), lambda i, j: (i, first + j, 0))],
        out_specs=pl.BlockSpec((1, rt, 2, A_HEADS, A_HEAD_DIM), lambda i, j: (i, j, 0, 0, 0)),
        out_shape=jax.ShapeDtypeStruct((b, keep, 2, A_HEADS, A_HEAD_DIM), F32),
        compiler_params=_params("parallel", "parallel"),
        name="kv_to_cache",
    )(kv)


def _attn_sample_kernel(window, dil, q_ref, kvn_ref, buf_ref, o_ref, l_ref, nbuf_ref):
    t_new = q_ref.shape[1]
    wb = buf_ref.shape[1]
    hd = A_HEAD_DIM
    scale = hd ** -0.5
    tb = lax.broadcasted_iota(jnp.int32, (t_new, wb), 0)
    eb = lax.broadcasted_iota(jnp.int32, (t_new, wb), 1)
    dist_b = wb + tb - eb
    ok_b = (dist_b <= window) & ((dist_b & (dil - 1)) == 0)
    tn = lax.broadcasted_iota(jnp.int32, (t_new, t_new), 0)
    en = lax.broadcasted_iota(jnp.int32, (t_new, t_new), 1)
    dist_n = tn - en
    ok_n = (dist_n >= 0) & (dist_n <= window) & ((dist_n & (dil - 1)) == 0)
    for h in range(A_HEADS):
        sl = slice(h * hd, (h + 1) * hd)
        sv = slice(A_WIDTH + h * hd, A_WIDTH + (h + 1) * hd)
        q = q_ref[0, :, sl].astype(BF16)
        s_b = lax.dot_general(q, buf_ref[0, :, 0, h, :].astype(BF16), _NT, preferred_element_type=F32) * scale
        s_n = lax.dot_general(q, kvn_ref[0, :, sl].astype(BF16), _NT, preferred_element_type=F32) * scale
        s_b = jnp.where(ok_b, s_b, -jnp.inf)
        s_n = jnp.where(ok_n, s_n, -jnp.inf)
        m = jnp.maximum(jnp.max(s_b, axis=-1, keepdims=True), jnp.max(s_n, axis=-1, keepdims=True))
        e_b = jnp.exp(s_b - m)
        e_n = jnp.exp(s_n - m)
        den = jnp.sum(e_b, axis=-1, keepdims=True) + jnp.sum(e_n, axis=-1, keepdims=True)
        acc = jnp.dot(e_b.astype(BF16), buf_ref[0, :, 1, h, :].astype(BF16), preferred_element_type=F32)
        acc = acc + jnp.dot(e_n.astype(BF16), kvn_ref[0, :, sv].astype(BF16), preferred_element_type=F32)
        o_ref[0, :, sl] = acc / den
        l_ref[0, :, sl] = jnp.broadcast_to(m + jnp.log(den), (t_new, hd))
        nbuf_ref[0, wb - t_new:wb, 0, h, :] = kvn_ref[0, :, sl]
        nbuf_ref[0, wb - t_new:wb, 1, h, :] = kvn_ref[0, :, sv]
    nbuf_ref[0, 0:wb - t_new] = buf_ref[0, t_new:wb]


def _attn_sample(q_all, kvn, buf, g):
    b, t_new, _ = q_all.shape
    wb = buf.shape[1]
    assert wb == WINDOWS[g] and wb >= t_new and t_new % 8 == 0
    assert DILATIONS[g] & (DILATIONS[g] - 1) == 0
    row = lambda i: (i, 0, 0)
    o_spec = pl.BlockSpec((1, t_new, A_WIDTH), row)
    c_spec = pl.BlockSpec((1, wb, 2, A_HEADS, A_HEAD_DIM), lambda i: (i, 0, 0, 0, 0))
    return pl.pallas_call(
        functools.partial(_attn_sample_kernel, WINDOWS[g], DILATIONS[g]),
        grid=(b,),
        in_specs=[pl.BlockSpec((1, t_new, A_WIDTH), lambda i: (i, 0, g)),
                  pl.BlockSpec((1, t_new, 2 * A_WIDTH), row),
                  c_spec],
        out_specs=[o_spec, o_spec, c_spec],
        out_shape=[jax.ShapeDtypeStruct((b, t_new, A_WIDTH), F32),
                   jax.ShapeDtypeStruct((b, t_new, A_WIDTH), F32),
                   jax.ShapeDtypeStruct(buf.shape, F32)],
        compiler_params=_params("parallel"),
        name="attn_sample_g%d" % g,
    )(q_all, kvn, buf)


def _merge_kernel(hm_ref, o0_ref, o1_ref, o2_ref, l0_ref, l1_ref, l2_ref, gate_ref, x_ref,
                  ga1_ref, sh2_ref, sc2_ref, wa_ref, wb_ref, wo_ref, gp1_ref, gp2_ref,
                  x1_ref, h2_ref):
    bb, lt, d = x_ref.shape
    rows = bb * lt
    two = lambda ref: ref[...].reshape(rows, ref.shape[-1])
    l0, l1, l2 = two(l0_ref), two(l1_ref), two(l2_ref)
    m = jnp.maximum(jnp.maximum(l0, l1), l2)
    e0, e1, e2 = jnp.exp(l0 - m), jnp.exp(l1 - m), jnp.exp(l2 - m)
    oa = (e0 * two(o0_ref) + e1 * two(o1_ref) + e2 * two(o2_ref)) / (e0 + e1 + e2)
    br_a = jnp.dot(two(hm_ref).astype(BF16), wa_ref[...], preferred_element_type=F32)
    br_b = jnp.dot(oa.astype(BF16), wb_ref[...], preferred_element_type=F32)
    gate = two(gate_ref)
    merged = jax.nn.sigmoid(gate[:, 0:d]) * br_a + jax.nn.sigmoid(gate[:, d:2 * d]) * br_b
    y = jnp.dot(merged.astype(BF16), wo_ref[...], preferred_element_type=F32)
    y = (_rms(y) * gp1_ref[...]).reshape(bb, lt, d)
    x1 = x_ref[...] + ga1_ref[...] * y
    x1_ref[...] = x1
    h2 = (_rms(x1) * gp2_ref[...].reshape(1, 1, d)) * (1.0 + sc2_ref[...]) + sh2_ref[...]
    h2_ref[...] = h2.reshape(rows, d).astype(h2_ref.dtype)


def _merge(hm, outs, lses, gates, x, ga1, sh2, sc2, w_a, w_b, w_out, g_post1, g_pre2, bb, lt):
    b, l, d = x.shape
    rows = bb * lt
    nl = l // lt
    idx = lambda i, j: (i, j, 0)
    vec = pl.BlockSpec((bb, 1, d), lambda i, j: (i, 0, 0))
    const = lambda i, j: (0, 0)
    aw = pl.BlockSpec((bb, lt, A_WIDTH), idx)
    return pl.pallas_call(
        _merge_kernel,
        grid=(b // bb, nl),
        in_specs=[pl.BlockSpec((bb, lt, M_WIDTH), idx), aw, aw, aw, aw, aw, aw,
                  pl.BlockSpec((bb, lt, 2 * d), idx), pl.BlockSpec((bb, lt, d), idx),
                  vec, vec, vec,
                  pl.BlockSpec(w_a.shape, const), pl.BlockSpec(w_b.shape, const),
                  pl.BlockSpec(w_out.shape, const),
                  pl.BlockSpec((1, d), const), pl.BlockSpec((1, d), const)],
        out_specs=[pl.BlockSpec((bb, lt, d), idx),
                   pl.BlockSpec((rows, d), lambda i, j: (i * nl + j, 0))],
        out_shape=[jax.ShapeDtypeStruct((b, l, d), F32),
                   jax.ShapeDtypeStruct((b * l, d), BF16)],
        compiler_params=_params("parallel", "parallel"),
        name="merge",
    )(hm, *outs, *lses, gates, x, ga1, sh2, sc2, w_a, w_b, w_out,
      g_post1.reshape(1, d), g_pre2.reshape(1, d))


_PICKED = -(2.0 ** 100)


def _topk_rows(x, k):
    n, m = x.shape
    ridx = lax.broadcasted_iota(jnp.int32, (n, m), 0).astype(F32)
    kidx = lax.broadcasted_iota(jnp.int32, (k, m), 0)

    assert k & (k - 1) == 0 and k <= 2 ** 20

    def body(r, carry):
        x, vals = carry
        mx = jnp.max(x, axis=0, keepdims=True)
        first = jnp.min(jnp.where(x == mx, ridx, float(n)), axis=0, keepdims=True)
        code = (1.0 + lax.convert_element_type(r, F32) * (1.0 / k)) * _PICKED
        x = jnp.where(ridx == first, code, x)
        vals = jnp.where(kidx == r, mx, vals)
        return x, vals

    x, vals = lax.fori_loop(0, k, body, (x, jnp.zeros((k, m), F32)))
    rank = jnp.where(x <= _PICKED, (x * (1.0 / _PICKED) - 1.0) * float(k), float(k))
    return vals, rank


def _peer_select_kernel(h2_ref, wqt_ref, k1_ref, k2_ref, cnt_ref, e1_ref, rk2_ref, e2_ref, qt_s):
    m = h2_ref.shape[0]
    k = P_TOPK
    qt_s[...] = lax.dot_general(wqt_ref[...], h2_ref[...], _NT, preferred_element_type=F32)
    k1 = k1_ref[...].astype(BF16)
    k2 = k2_ref[...].astype(BF16)
    i16 = lax.broadcasted_iota(jnp.int32, (k, m), 0)
    i8 = lax.broadcasted_iota(jnp.int32, (8, m), 0)
    pos = [(i16 * k).astype(F32), (i8 * k + 1).astype(F32)]
    pos += [(i8 * k + jj).astype(F32) for jj in range(2, 8)]
    pos += [(i8 + 8).astype(F32)]
    pos = jnp.concatenate(pos, axis=0)
    ncand = pos.shape[0]

    def per_head(h, _):
        qa = qt_s[pl.ds(pl.multiple_of(h * P_QDIM, P_QDIM), P_HALF), :].astype(BF16)
        qb = qt_s[pl.ds(pl.multiple_of(h * P_QDIM + P_HALF, P_HALF), P_HALF), :].astype(BF16)
        s1 = jnp.dot(k1, qa, preferred_element_type=F32)
        s2 = jnp.dot(k2, qb, preferred_element_type=F32)
        v1, rk1 = _topk_rows(s1, k)
        v2, rk2 = _topk_rows(s2, k)
        blocks = [v1 + v2[0:1], v1[0:8] + v2[1:2]]
        for jj in range(2, 8):
            blocks.append(jnp.where(i8 < k // (jj + 1), v1[0:8] + v2[jj:jj + 1], -jnp.inf))
        blocks.append(v1[0:1] + v2[8:16])
        cand = jnp.concatenate(blocks, axis=0)
        cmax = v1[0:1] + v2[0:1]

        def pick(r, carry):
            cand, z = carry
            mx = jnp.max(cand, axis=0, keepdims=True)
            first = jnp.min(jnp.where(cand == mx, pos, 1e9), axis=0, keepdims=True)
            cand = jnp.where(pos == first, _PICKED, cand)
            return cand, z + jnp.exp(mx - cmax)

        cand, z = lax.fori_loop(0, k, pick, (cand, jnp.zeros((1, m), F32)))
        selm = jnp.where(cand == _PICKED, 1.0, 0.0)
        low = selm[16:24]
        for jj in range(2, 8):
            low = low + selm[8 + 8 * jj:16 + 8 * jj]
        cnt16 = selm[0:16] + jnp.concatenate([low, jnp.zeros((8, m), F32)], axis=0)
        cnt16 = cnt16 + jnp.where(i16 == 0, jnp.sum(selm[72:80], axis=0, keepdims=True), 0.0)
        cnt = jnp.zeros_like(s1)
        for r in range(k):
            cnt = cnt + jnp.where(rk1 == float(r), cnt16[r:r + 1], 0.0)
        cnt_ref[h] = cnt
        e1_ref[h] = jnp.exp(s1 - v1[0:1]) / z
        rk2_ref[h] = rk2
        e2_ref[h] = jnp.exp(s2 - v2[0:1])
        return 0

    lax.fori_loop(0, P_HEADS, per_head, 0)


def _peer_select(h2, wq_t, k1, k2):
    t, d = h2.shape
    tm = min(t, SELECT_TOKENS)
    sel_spec = pl.BlockSpec((P_HEADS, P_NKEYS, tm), lambda i: (0, 0, i))
    sel_shape = jax.ShapeDtypeStruct((P_HEADS, P_NKEYS, t), F32)
    return pl.pallas_call(
        _peer_select_kernel,
        grid=(t // tm,),
        in_specs=[pl.BlockSpec((tm, d), lambda i: (i, 0)),
                  pl.BlockSpec(wq_t.shape, lambda i: (0, 0)),
                  pl.BlockSpec(k1.shape, lambda i: (0, 0)),
                  pl.BlockSpec(k2.shape, lambda i: (0, 0))],
        out_specs=[sel_spec] * 4,
        out_shape=[sel_shape] * 4,
        scratch_shapes=[pltpu.VMEM((P_HEADS * P_QDIM, tm), F32)],
        compiler_params=_params("parallel"),
        name="peer_select",
    )(h2, wq_t, k1, k2)


E_SUB = 8
E_TILE = E_SUB * P_NKEYS


def _gelu(x):
    return 0.5 * x * (1.0 + lax.erf(x * math.sqrt(0.5)))


def _peer_dense_kernel(h2_ref, u_ref, vt_ref, cnt_ref, e1_ref, rk2_ref, e2_ref,
                       x1_ref, ga2_ref, gp2_ref, o_ref, s_s, a_s, acc_s):
    kk = pl.program_id(2)
    bb, lt, d = x1_ref.shape
    m = bb * lt

    @pl.when(kk == 0)
    def _zero():
        acc_s[...] = jnp.zeros_like(acc_s)

    s_s[...] = lax.dot_general(u_ref[...], h2_ref[...], _NT, preferred_element_type=F32)
    for i in range(E_SUB):
        rows = slice(i * P_NKEYS, (i + 1) * P_NKEYS)
        w = jnp.zeros((P_NKEYS, m), F32)
        for h in range(P_HEADS):
            picked = rk2_ref[h] < cnt_ref[h, i:i + 1, :]
            w = w + jnp.where(picked, e2_ref[h], 0.0) * e1_ref[h, i:i + 1, :]
        a_s[rows, :] = (w * _gelu(s_s[rows, :])).astype(BF16)
    acc_s[...] += jnp.dot(vt_ref[...], a_s[...], preferred_element_type=F32)

    @pl.when(kk == pl.num_programs(2) - 1)
    def _finish():
        f = acc_s[...].T
        y = (_rms(f) * gp2_ref[...]).reshape(bb, lt, d)
        o_ref[...] = x1_ref[...] + ga2_ref[...] * y


def _peer_dense(h2, u, v_t, sel, x1, ga2, g_post2, bb, lt):
    b, l, d = x1.shape
    cnt, e1, rk2, e2 = sel
    m = bb * lt
    nl = l // lt
    ne = u.shape[0] // E_TILE
    tok = lambda i, j, kk: i * nl + j
    sub = pl.BlockSpec((P_HEADS, E_SUB, m), lambda i, j, kk: (0, kk, tok(i, j, kk)))
    full = pl.BlockSpec((P_HEADS, P_NKEYS, m), lambda i, j, kk: (0, 0, tok(i, j, kk)))
    xs = pl.BlockSpec((bb, lt, d), lambda i, j, kk: (i, j, 0))
    return pl.pallas_call(
        _peer_dense_kernel,
        grid=(b // bb, nl, ne),
        in_specs=[pl.BlockSpec((m, d), lambda i, j, kk: (tok(i, j, kk), 0)),
                  pl.BlockSpec((E_TILE, d), lambda i, j, kk: (kk, 0)),
                  pl.BlockSpec((d, E_TILE), lambda i, j, kk: (0, kk)),
                  sub, sub, full, full, xs,
                  pl.BlockSpec((bb, 1, d), lambda i, j, kk: (i, 0, 0)),
                  pl.BlockSpec((1, d), lambda i, j, kk: (0, 0))],
        out_specs=xs,
        out_shape=jax.ShapeDtypeStruct((b, l, d), F32),
        scratch_shapes=[pltpu.VMEM((E_TILE, m), F32),
                        pltpu.VMEM((E_TILE, m), BF16),
                        pltpu.VMEM((d, m), F32)],
        compiler_params=_params("parallel", "parallel", "arbitrary"),
        name="peer_dense",
    )(h2, u, v_t, cnt, e1, rk2, e2, x1, ga2, g_post2.reshape(1, d))


def _layer(x, mod, state, prm, bb, lt, chunk, tm_proj):
    b, l, d = x.shape
    t = b * l
    sh1, sc1, ga1, sh2, sc2, ga2 = (mod[:, None, i * d:(i + 1) * d] for i in range(6))

    h = _prenorm(x, sh1, sc1, prm["g_pre1"], bb, lt).reshape(t, d)
    (mparts,) = _multi_matmul(h, [prm["w_m"]], tm_proj, "proj_mlstm")
    q_all, kv0, kv1, kv2 = _multi_matmul(h, [prm["w_q"]] + prm["w_kv"], min(tm_proj, 256), "proj_attn")
    (gates,) = _multi_matmul(h, [prm["w_g"]], tm_proj, "proj_gate")

    mstate = None
    if state is not None:
        mstate = (state[3], state[4], jnp.broadcast_to(state[5][..., None], state[5].shape + (LANE,)),
                  state[6])
    hm, c_new, n_new, m_new, conv_new = _mlstm(
        mparts.reshape(b, l, -1), prm["conv_w"], prm["conv_b"], prm["gate_b"], prm["mh_norm_g"],
        mstate, chunk)

    q_all = q_all.reshape(b, l, -1)
    kvs = [kv.reshape(b, l, 2 * A_WIDTH) for kv in (kv0, kv1, kv2)]
    outs, lses, bufs = [], [], []
    for g in range(N_GROUPS):
        if state is None:
            o, lse = _attn_prompt(q_all, kvs[g], g, PROMPT_HEADS_PER_STEP[g])
            nbuf = _kv_to_cache(kvs[g], min(WINDOWS[g], l))
        else:
            o, lse, nbuf = _attn_sample(q_all, kvs[g], state[g], g)
        outs.append(o)
        lses.append(lse)
        bufs.append(nbuf)

    x1, h2 = _merge(hm, outs, lses, gates.reshape(b, l, 2 * d), x, ga1, sh2, sc2,
                    prm["w_a"], prm["w_b"], prm["w_out"], prm["g_post1"], prm["g_pre2"], bb, lt)
    sel = _peer_select(h2, prm["wq_t"], prm["peer_k1"], prm["peer_k2"])
    y = _peer_dense(h2, prm["peer_u"], prm["peer_vt"], sel, x1, ga2, prm["g_post2"], bb, lt)
    return y, (bufs[0], bufs[1], bufs[2], c_new, n_new, m_new[:, :, 0], conv_new)


def kernel(x_prompt, x_sample, cache_win0_kv, cache_win1_kv, cache_win2_kv, state_mlstm_C, state_mlstm_n, state_mlstm_m, state_conv, c_prompt, c_sample, w_ada, b_ada, g_pre1, g_post1, g_pre2, g_post2, w_in, conv_w, conv_b, b_igate, b_fgate, mh_norm_g, w_a, w_b, w_out, peer_wq, peer_k1, peer_k2, peer_u, peer_v):
    depth = w_in.shape[0]
    d = x_prompt.shape[-1]
    nb_p = x_prompt.shape[0]
    yp, ys = x_prompt, x_sample
    new_p, new_s = [], []
    for layer in range(depth):
        w = w_in[layer]
        o_ig = QK_WIDTH + 2 * M_WIDTH
        o_at = o_ig + 2 * M_HEADS
        o_gt = o_at + 3 * N_GROUPS * A_WIDTH
        prm = {
            "g_pre1": g_pre1[layer], "g_post1": g_post1[layer],
            "g_pre2": g_pre2[layer], "g_post2": g_post2[layer],
            "w_m": jnp.concatenate(
                [w[:, :o_at], jnp.zeros((d, GATE_PAD - 2 * M_HEADS), F32)], axis=1).astype(BF16),
            "w_q": jnp.concatenate(
                [w[:, o_at + 3 * A_WIDTH * g:o_at + 3 * A_WIDTH * g + A_WIDTH] for g in range(N_GROUPS)],
                axis=1).astype(BF16),
            "w_kv": [w[:, o_at + 3 * A_WIDTH * g + A_WIDTH:o_at + 3 * A_WIDTH * (g + 1)].astype(BF16)
                     for g in range(N_GROUPS)],
            "w_g": w[:, o_gt:].astype(BF16),
            "conv_w": conv_w[layer], "conv_b": conv_b[layer],
            "gate_b": jnp.concatenate(
                [b_igate[layer], b_fgate[layer], jnp.zeros((GATE_PAD - 2 * M_HEADS,), F32)]).reshape(1, GATE_PAD),
            "mh_norm_g": mh_norm_g[layer],
            "w_a": w_a[layer].astype(BF16), "w_b": w_b[layer].astype(BF16),
            "w_out": w_out[layer].astype(BF16),
            "wq_t": peer_wq[layer].T.astype(BF16),
            "peer_k1": peer_k1[layer], "peer_k2": peer_k2[layer],
            "peer_u": peer_u[layer].astype(BF16),
            "peer_vt": peer_v[layer].T.astype(BF16),
        }
        mod = _adaln(jnp.concatenate([c_prompt, c_sample], axis=0), w_ada[layer], b_ada[layer])
        state = (cache_win0_kv[layer], cache_win1_kv[layer], cache_win2_kv[layer],
                 state_mlstm_C[layer], state_mlstm_n[layer], state_mlstm_m[layer], state_conv[layer])
        lp = yp.shape[1]
        yp, sp = _layer(yp, mod[:nb_p], None, prm, 1, 512, math.gcd(lp, MLSTM_CHUNK), 512)
        bs, ls = ys.shape[0], ys.shape[1]
        ys, ss = _layer(ys, mod[nb_p:], state, prm, bs, ls, ls, bs * ls)
        new_p.append(sp)
        new_s.append(ss)
    stacked_p = tuple(jnp.stack(z) for z in zip(*new_p))
    stacked_s = tuple(jnp.stack(z) for z in zip(*new_s))
    return (yp, ys) + stacked_p + stacked_s
```

```python
import functools
import math

import jax
import jax.numpy as jnp
from jax import lax
from jax.experimental import pallas as pl
from jax.experimental.pallas import tpu as pltpu

F32 = jnp.float32
BF16 = jnp.bfloat16
HIGHEST = lax.Precision.HIGHEST

RMS_EPS = 1e-6
M_HEADS = 4
M_HEAD_DIM = 128
M_WIDTH = M_HEADS * M_HEAD_DIM
CONV_WIDTH = 4
QK_WIDTH = 2 * M_WIDTH
A_HEADS = 4
A_HEAD_DIM = 128
A_WIDTH = A_HEADS * A_HEAD_DIM
WINDOWS = (128, 512, 2048)
DILATIONS = (1, 4, 16)
N_GROUPS = 3
P_HEADS = 8
P_QDIM = 256
P_HALF = P_QDIM // 2
P_NKEYS = 128
P_TOPK = 16

LANE = 128
GATE_PAD = LANE
MLSTM_CHUNK = 256
ATTN_BLOCK = 128
SELECT_TOKENS = 2 * LANE
PROMPT_HEADS_PER_STEP = (4, 1, 1)
VMEM_LIMIT = 48 * 1024 * 1024

_NT = (((1,), (1,)), ((), ()))
_TN = (((0,), (0,)), ((), ()))


def _params(*sem):
    return pltpu.CompilerParams(dimension_semantics=sem, vmem_limit_bytes=VMEM_LIMIT)


def _rms(x):
    return x * lax.rsqrt(jnp.mean(x * x, axis=-1, keepdims=True) + RMS_EPS)


def _adaln_kernel(c_ref, w_ref, b_ref, o_ref):
    c = c_ref[...]
    a = (c * jax.nn.sigmoid(c)).astype(BF16)
    o_ref[...] = jnp.dot(a, w_ref[...].astype(BF16), preferred_element_type=F32) + b_ref[...]


def _adaln(c_all, w_ada, b_ada):
    bc, d = c_all.shape
    n = w_ada.shape[1]
    tn = 1024
    return pl.pallas_call(
        _adaln_kernel,
        grid=(n // tn,),
        in_specs=[pl.BlockSpec((bc, d), lambda j: (0, 0)),
                  pl.BlockSpec((d, tn), lambda j: (0, j)),
                  pl.BlockSpec((1, tn), lambda j: (0, j))],
        out_specs=pl.BlockSpec((bc, tn), lambda j: (0, j)),
        out_shape=jax.ShapeDtypeStruct((bc, n), F32),
        compiler_params=_params("parallel"),
        name="adaln",
    )(c_all, w_ada, b_ada.reshape(1, n))


def _prenorm_kernel(x_ref, sh_ref, sc_ref, g_ref, o_ref):
    y = _rms(x_ref[...]) * g_ref[...]
    o_ref[...] = (y * (1.0 + sc_ref[...]) + sh_ref[...]).astype(o_ref.dtype)


def _prenorm(x, shift, scale, g, bb, lt):
    b, l, d = x.shape
    row = pl.BlockSpec((bb, lt, d), lambda i, j: (i, j, 0))
    vec = pl.BlockSpec((bb, 1, d), lambda i, j: (i, 0, 0))
    return pl.pallas_call(
        _prenorm_kernel,
        grid=(b // bb, l // lt),
        in_specs=[row, vec, vec, pl.BlockSpec((1, 1, d), lambda i, j: (0, 0, 0))],
        out_specs=row,
        out_shape=jax.ShapeDtypeStruct((b, l, d), BF16),
        compiler_params=_params("parallel", "parallel"),
        name="prenorm",
    )(x, shift, scale, g.reshape(1, 1, d))


def _mm_kernel(n, a_ref, *refs):
    a = a_ref[...]
    for w_ref, o_ref in zip(refs[:n], refs[n:]):
        o_ref[...] = jnp.dot(a, w_ref[...], preferred_element_type=F32).astype(o_ref.dtype)


def _multi_matmul(a, ws, tm, name):
    m, k = a.shape
    n = len(ws)
    return pl.pallas_call(
        functools.partial(_mm_kernel, n),
        grid=(m // tm,),
        in_specs=[pl.BlockSpec((tm, k), lambda i: (i, 0))]
        + [pl.BlockSpec(w.shape, lambda i: (0, 0)) for w in ws],
        out_specs=[pl.BlockSpec((tm, w.shape[1]), lambda i: (i, 0)) for w in ws],
        out_shape=[jax.ShapeDtypeStruct((m, w.shape[1]), F32) for w in ws],
        compiler_params=_params("parallel"),
        name=name,
    )(a, *ws)


def _log_sigmoid(x):
    return jnp.minimum(x, 0.0) - jnp.log1p(jnp.exp(-jnp.abs(x)))


def _mlstm_kernel(has_state, c, mp_ref, cw_ref, cb_ref, gb_ref, mg_ref, *refs):
    if has_state:
        c0_ref, n0_ref, m0_ref, cv0_ref = refs[:4]
        refs = refs[4:]
    hm_ref, c_ref, n_ref, m_ref, cv_ref, ubuf = refs
    j = pl.program_id(1)
    hd = M_HEAD_DIM
    pad = 8

    @pl.when(j == 0)
    def _init():
        if has_state:
            c_ref[...] = c0_ref[...]
            n_ref[...] = n0_ref[...]
            m_ref[...] = m0_ref[...]
            ubuf[pad - 3:pad, :] = cv0_ref[0]
        else:
            c_ref[...] = jnp.zeros_like(c_ref)
            n_ref[...] = jnp.zeros_like(n_ref)
            m_ref[...] = jnp.zeros_like(m_ref)
            ubuf[pad - 3:pad, :] = jnp.zeros((3, QK_WIDTH), F32)

    ubuf[pad:pad + c, :] = mp_ref[0, :, 0:QK_WIDTH]
    acc = cb_ref[...] + cw_ref[0:1, :] * ubuf[pad - 3:pad - 3 + c, :]
    for t in range(1, CONV_WIDTH):
        acc = acc + cw_ref[t:t + 1, :] * ubuf[pad - 3 + t:pad - 3 + t + c, :]
    tail = ubuf[pad + c - 3:pad + c, :]
    ubuf[pad - 3:pad, :] = tail
    cv_ref[0] = tail
    qk = acc * jax.nn.sigmoid(acc)

    lane = lax.broadcasted_iota(jnp.int32, (c, GATE_PAD), 1)
    graw = mp_ref[0, :, QK_WIDTH + 2 * M_WIDTH:QK_WIDTH + 2 * M_WIDTH + GATE_PAD] + gb_ref[...]
    gates = jnp.where(lane < M_HEADS, graw, jnp.where(lane < 2 * M_HEADS, _log_sigmoid(graw), 0.0))
    row = lax.broadcasted_iota(jnp.int32, (c, c), 0)
    col = lax.broadcasted_iota(jnp.int32, (c, c), 1)
    causal = col <= row
    tril = jnp.where(causal, 1.0, 0.0).astype(F32)
    cum = jnp.dot(tril, gates, precision=HIGHEST, preferred_element_type=F32)

    for h in range(M_HEADS):
        sl = slice(h * hd, (h + 1) * hd)
        qh = qk[:, sl].astype(BF16)
        kh = qk[:, M_WIDTH + h * hd:M_WIDTH + (h + 1) * hd] * (hd ** -0.5)
        vh = mp_ref[0, :, QK_WIDTH + h * hd:QK_WIDTH + (h + 1) * hd].astype(BF16)
        og = mp_ref[0, :, QK_WIDTH + M_WIDTH + h * hd:QK_WIDTH + M_WIDTH + (h + 1) * hd]
        i_col = gates[:, h:h + 1]
        b_col = cum[:, M_HEADS + h:M_HEADS + h + 1]
        lhs = jnp.where(lane == 0, b_col, jnp.where(lane == 1, 1.0, 0.0))
        rhs = jnp.where(lane == 0, 1.0, jnp.where(lane == 1, i_col - b_col, 0.0))
        dmat = lax.dot_general(lhs, rhs, _NT, precision=HIGHEST, preferred_element_type=F32)
        dmat = jnp.where(causal, dmat, -jnp.inf)
        m_prev = m_ref[0, h:h + 1, 0:1]
        inter = b_col + m_prev
        mt = jnp.maximum(inter, jnp.max(dmat, axis=-1, keepdims=True))
        w_in = jnp.exp(dmat - mt)
        w_prev = jnp.exp(inter - mt)
        s = lax.dot_general(qh, kh.astype(BF16), _NT, preferred_element_type=F32) * w_in
        cmat = c_ref[0, h]
        nrow = n_ref[0, h:h + 1, :]
        num = jnp.dot(s.astype(BF16), vh, preferred_element_type=F32) + w_prev * jnp.dot(
            qh, cmat.astype(BF16), preferred_element_type=F32)
        den = jnp.sum(s, axis=-1, keepdims=True) + w_prev * jnp.sum(
            qk[:, sl] * nrow, axis=-1, keepdims=True)
        hh = num / jnp.maximum(jnp.abs(den), jnp.exp(-mt))
        m_new = mt[c - 1:c, :]
        b_last = b_col[c - 1:c, :]
        w_end = jnp.exp(b_last - b_col + i_col - m_new)
        decay = jnp.exp(b_last + m_prev - m_new)
        kw = kh * w_end
        c_ref[0, h] = decay * cmat + lax.dot_general(kw.astype(BF16), vh, _TN, preferred_element_type=F32)
        n_ref[0, h:h + 1, :] = decay * nrow + jnp.sum(kw, axis=0, keepdims=True)
        m_ref[0, h:h + 1, :] = jnp.broadcast_to(m_new, (1, LANE))
        hh = hh * jax.nn.sigmoid(og)
        hm_ref[0, :, sl] = _rms(hh) * mg_ref[:, sl]


def _mlstm(mparts, conv_w, conv_b, gate_b, mh_g, state, c):
    b, l, width = mparts.shape
    has_state = state is not None
    nj = l // c
    per_b4 = lambda i, j: (i, 0, 0, 0)
    per_b3 = lambda i, j: (i, 0, 0)
    const2 = lambda i, j: (0, 0)
    in_specs = [pl.BlockSpec((1, c, width), lambda i, j: (i, j, 0)),
                pl.BlockSpec(conv_w.shape, const2),
                pl.BlockSpec((1, QK_WIDTH), const2),
                pl.BlockSpec((1, GATE_PAD), const2),
                pl.BlockSpec((1, M_WIDTH), const2)]
    args = [mparts, conv_w, conv_b.reshape(1, QK_WIDTH), gate_b, mh_g.reshape(1, M_WIDTH)]
    state_specs = [pl.BlockSpec((1, M_HEADS, M_HEAD_DIM, M_HEAD_DIM), per_b4),
                   pl.BlockSpec((1, M_HEADS, M_HEAD_DIM), per_b3),
                   pl.BlockSpec((1, M_HEADS, LANE), per_b3),
                   pl.BlockSpec((1, CONV_WIDTH - 1, QK_WIDTH), per_b3)]
    if has_state:
        in_specs += state_specs
        args += list(state)
    return pl.pallas_call(
        functools.partial(_mlstm_kernel, has_state, c),
        grid=(b, nj),
        in_specs=in_specs,
        out_specs=[pl.BlockSpec((1, c, M_WIDTH), lambda i, j: (i, j, 0))] + state_specs,
        out_shape=[jax.ShapeDtypeStruct((b, l, M_WIDTH), F32),
                   jax.ShapeDtypeStruct((b, M_HEADS, M_HEAD_DIM, M_HEAD_DIM), F32),
                   jax.ShapeDtypeStruct((b, M_HEADS, M_HEAD_DIM), F32),
                   jax.ShapeDtypeStruct((b, M_HEADS, LANE), F32),
                   jax.ShapeDtypeStruct((b, CONV_WIDTH - 1, QK_WIDTH), F32)],
        scratch_shapes=[pltpu.VMEM((8 + c, QK_WIDTH), F32)],
        compiler_params=_params("parallel", "arbitrary"),
        name="mlstm",
    )(*args)


def _attn_prompt_kernel(dil, hps, use_prev, q_ref, kc_ref, vc_ref, *rest):
    if use_prev:
        kp_ref, vp_ref, o_ref, l_ref = rest
    else:
        o_ref, l_ref = rest
    n = pl.program_id(2)
    blk, hd = ATTN_BLOCK, A_HEAD_DIM
    scale = hd ** -0.5
    row = lax.broadcasted_iota(jnp.int32, (blk, blk), 0)
    col = lax.broadcasted_iota(jnp.int32, (blk, blk), 1)
    cur_ok = col <= row
    prev_ok = (col - row) >= jnp.where(n > 0, 0, blk)

    def residue(r):
        rows = pl.ds(r, blk, stride=dil) if dil > 1 else pl.ds(0, blk)
        for h in range(hps):
            sl = slice(h * hd, (h + 1) * hd)
            q = q_ref[0, rows, sl].astype(BF16)
            s_c = lax.dot_general(q, kc_ref[0, rows, sl].astype(BF16), _NT, preferred_element_type=F32) * scale
            s_c = jnp.where(cur_ok, s_c, -jnp.inf)
            m = jnp.max(s_c, axis=-1, keepdims=True)
            if use_prev:
                s_p = lax.dot_general(q, kp_ref[0, rows, sl].astype(BF16), _NT, preferred_element_type=F32) * scale
                s_p = jnp.where(prev_ok, s_p, -jnp.inf)
                m = jnp.maximum(m, jnp.max(s_p, axis=-1, keepdims=True))
            e_c = jnp.exp(s_c - m)
            den = jnp.sum(e_c, axis=-1, keepdims=True)
            acc = jnp.dot(e_c.astype(BF16), vc_ref[0, rows, sl].astype(BF16), preferred_element_type=F32)
            if use_prev:
                e_p = jnp.exp(s_p - m)
                den = den + jnp.sum(e_p, axis=-1, keepdims=True)
                acc = acc + jnp.dot(e_p.astype(BF16), vp_ref[0, rows, sl].astype(BF16), preferred_element_type=F32)
            o_ref[0, rows, sl] = acc / den
            l_ref[0, rows, sl] = jnp.broadcast_to(m + jnp.log(den), (blk, hd))

    if dil == 1:
        residue(0)
    else:
        def body(r, carry):
            residue(r)
            return carry
        lax.fori_loop(0, dil, body, 0)


def _attn_prompt(q_all, kv, g, hps):
    b, s, _ = q_all.shape
    dil = DILATIONS[g]
    span = dil * ATTN_BLOCK
    assert WINDOWS[g] // dil == ATTN_BLOCK and s % span == 0 and A_HEADS % hps == 0
    nb = s // span
    nh = A_HEADS // hps
    w = hps * A_HEAD_DIM
    use_prev = nb > 1
    cur = lambda off: pl.BlockSpec((1, span, w), lambda i, hs, n: (i, n, off + hs))
    prev = lambda off: pl.BlockSpec((1, span, w), lambda i, hs, n: (i, jnp.maximum(n - 1, 0), off + hs))
    in_specs = [cur(g * nh), cur(0), cur(nh)]
    args = [q_all, kv, kv]
    if use_prev:
        in_specs += [prev(0), prev(nh)]
        args += [kv, kv]
    return pl.pallas_call(
        functools.partial(_attn_prompt_kernel, dil, hps, use_prev),
        grid=(b, nh, nb),
        in_specs=in_specs,
        out_specs=[cur(0), cur(0)],
        out_shape=[jax.ShapeDtypeStruct((b, s, A_WIDTH), F32)] * 2,
        compiler_params=_params("parallel", "parallel", "arbitrary"),
        name="attn_prompt_g%d" % g,
    )(*args)


def _kv_to_cache_kernel(kv_ref, o_ref):
    hd = A_HEAD_DIM
    for j in range(2):
        for h in range(A_HEADS):
            o_ref[0, :, j, h, :] = kv_ref[0, :, j * A_WIDTH + h * hd:j * A_WIDTH + (h + 1) * hd]


def _kv_to_cache(kv, keep):
    b, s, _ = kv.shape
    rt = min(keep, 512)
    assert keep % rt == 0 and (s - keep) % rt == 0
    first = (s - keep) // rt
    return pl.pallas_call(
        _kv_to_cache_kernel,
        grid=(b, keep // rt),
        in_specs=[pl.BlockSpec((1, rt, 2 * A_WIDTH), lambda i, j: (i, first + j, 0))],
        out_specs=pl.BlockSpec((1, rt, 2, A_HEADS, A_HEAD_DIM), lambda i, j: (i, j, 0, 0, 0)),
        out_shape=jax.ShapeDtypeStruct((b, keep, 2, A_HEADS, A_HEAD_DIM), F32),
        compiler_params=_params("parallel", "parallel"),
        name="kv_to_cache",
    )(kv)


def _attn_sample_kernel(window, dil, q_ref, kvn_ref, buf_ref, o_ref, l_ref, nbuf_ref):
    t_new = q_ref.shape[1]
    wb = buf_ref.shape[1]
    hd = A_HEAD_DIM
    scale = hd ** -0.5
    tb = lax.broadcasted_iota(jnp.int32, (t_new, wb), 0)
    eb = lax.broadcasted_iota(jnp.int32, (t_new, wb), 1)
    dist_b = wb + tb - eb
    ok_b = (dist_b <= window) & ((dist_b & (dil - 1)) == 0)
    tn = lax.broadcasted_iota(jnp.int32, (t_new, t_new), 0)
    en = lax.broadcasted_iota(jnp.int32, (t_new, t_new), 1)
    dist_n = tn - en
    ok_n = (dist_n >= 0) & (dist_n <= window) & ((dist_n & (dil - 1)) == 0)
    for h in range(A_HEADS):
        sl = slice(h * hd, (h + 1) * hd)
        sv = slice(A_WIDTH + h * hd, A_WIDTH + (h + 1) * hd)
        q = q_ref[0, :, sl].astype(BF16)
        s_b = lax.dot_general(q, buf_ref[0, :, 0, h, :].astype(BF16), _NT, preferred_element_type=F32) * scale
        s_n = lax.dot_general(q, kvn_ref[0, :, sl].astype(BF16), _NT, preferred_element_type=F32) * scale
        s_b = jnp.where(ok_b, s_b, -jnp.inf)
        s_n = jnp.where(ok_n, s_n, -jnp.inf)
        m = jnp.maximum(jnp.max(s_b, axis=-1, keepdims=True), jnp.max(s_n, axis=-1, keepdims=True))
        e_b = jnp.exp(s_b - m)
        e_n = jnp.exp(s_n - m)
        den = jnp.sum(e_b, axis=-1, keepdims=True) + jnp.sum(e_n, axis=-1, keepdims=True)
        acc = jnp.dot(e_b.astype(BF16), buf_ref[0, :, 1, h, :].astype(BF16), preferred_element_type=F32)
        acc = acc + jnp.dot(e_n.astype(BF16), kvn_ref[0, :, sv].astype(BF16), preferred_element_type=F32)
        o_ref[0, :, sl] = acc / den
        l_ref[0, :, sl] = jnp.broadcast_to(m + jnp.log(den), (t_new, hd))
        nbuf_ref[0, wb - t_new:wb, 0, h, :] = kvn_ref[0, :, sl]
        nbuf_ref[0, wb - t_new:wb, 1, h, :] = kvn_ref[0, :, sv]
    nbuf_ref[0, 0:wb - t_new] = buf_ref[0, t_new:wb]


def _attn_sample(q_all, kvn, buf, g):
    b, t_new, _ = q_all.shape
    wb = buf.shape[1]
    assert wb == WINDOWS[g] and wb >= t_new and t_new % 8 == 0
    assert DILATIONS[g] & (DILATIONS[g] - 1) == 0
    row = lambda i: (i, 0, 0)
    o_spec = pl.BlockSpec((1, t_new, A_WIDTH), row)
    c_spec = pl.BlockSpec((1, wb, 2, A_HEADS, A_HEAD_DIM), lambda i: (i, 0, 0, 0, 0))
    return pl.pallas_call(
        functools.partial(_attn_sample_kernel, WINDOWS[g], DILATIONS[g]),
        grid=(b,),
        in_specs=[pl.BlockSpec((1, t_new, A_WIDTH), lambda i: (i, 0, g)),
                  pl.BlockSpec((1, t_new, 2 * A_WIDTH), row),
                  c_spec],
        out_specs=[o_spec, o_spec, c_spec],
        out_shape=[jax.ShapeDtypeStruct((b, t_new, A_WIDTH), F32),
                   jax.ShapeDtypeStruct((b, t_new, A_WIDTH), F32),
                   jax.ShapeDtypeStruct(buf.shape, F32)],
        compiler_params=_params("parallel"),
        name="attn_sample_g%d" % g,
    )(q_all, kvn, buf)


def _merge_kernel(hm_ref, o0_ref, o1_ref, o2_ref, l0_ref, l1_ref, l2_ref, gate_ref, x_ref,
                  ga1_ref, sh2_ref, sc2_ref, wa_ref, wb_ref, wo_ref, gp1_ref, gp2_ref,
                  x1_ref, h2_ref):
    bb, lt, d = x_ref.shape
    rows = bb * lt
    two = lambda ref: ref[...].reshape(rows, ref.shape[-1])
    l0, l1, l2 = two(l0_ref), two(l1_ref), two(l2_ref)
    m = jnp.maximum(jnp.maximum(l0, l1), l2)
    e0, e1, e2 = jnp.exp(l0 - m), jnp.exp(l1 - m), jnp.exp(l2 - m)
    oa = (e0 * two(o0_ref) + e1 * two(o1_ref) + e2 * two(o2_ref)) / (e0 + e1 + e2)
    br_a = jnp.dot(two(hm_ref).astype(BF16), wa_ref[...], preferred_element_type=F32)
    br_b = jnp.dot(oa.astype(BF16), wb_ref[...], preferred_element_type=F32)
    gate = two(gate_ref)
    merged = jax.nn.sigmoid(gate[:, 0:d]) * br_a + jax.nn.sigmoid(gate[:, d:2 * d]) * br_b
    y = jnp.dot(merged.astype(BF16), wo_ref[...], preferred_element_type=F32)
    y = (_rms(y) * gp1_ref[...]).reshape(bb, lt, d)
    x1 = x_ref[...] + ga1_ref[...] * y
    x1_ref[...] = x1
    h2 = (_rms(x1) * gp2_ref[...].reshape(1, 1, d)) * (1.0 + sc2_ref[...]) + sh2_ref[...]
    h2_ref[...] = h2.reshape(rows, d).astype(h2_ref.dtype)


def _merge(hm, outs, lses, gates, x, ga1, sh2, sc2, w_a, w_b, w_out, g_post1, g_pre2, bb, lt):
    b, l, d = x.shape
    rows = bb * lt
    nl = l // lt
    idx = lambda i, j: (i, j, 0)
    vec = pl.BlockSpec((bb, 1, d), lambda i, j: (i, 0, 0))
    const = lambda i, j: (0, 0)
    aw = pl.BlockSpec((bb, lt, A_WIDTH), idx)
    return pl.pallas_call(
        _merge_kernel,
        grid=(b // bb, nl),
        in_specs=[pl.BlockSpec((bb, lt, M_WIDTH), idx), aw, aw, aw, aw, aw, aw,
                  pl.BlockSpec((bb, lt, 2 * d), idx), pl.BlockSpec((bb, lt, d), idx),
                  vec, vec, vec,
                  pl.BlockSpec(w_a.shape, const), pl.BlockSpec(w_b.shape, const),
                  pl.BlockSpec(w_out.shape, const),
                  pl.BlockSpec((1, d), const), pl.BlockSpec((1, d), const)],
        out_specs=[pl.BlockSpec((bb, lt, d), idx),
                   pl.BlockSpec((rows, d), lambda i, j: (i * nl + j, 0))],
        out_shape=[jax.ShapeDtypeStruct((b, l, d), F32),
                   jax.ShapeDtypeStruct((b * l, d), BF16)],
        compiler_params=_params("parallel", "parallel"),
        name="merge",
    )(hm, *outs, *lses, gates, x, ga1, sh2, sc2, w_a, w_b, w_out,
      g_post1.reshape(1, d), g_pre2.reshape(1, d))


_PICKED = -(2.0 ** 100)


def _topk_rows(x, k):
    n, m = x.shape
    ridx = lax.broadcasted_iota(jnp.int32, (n, m), 0).astype(F32)
    kidx = lax.broadcasted_iota(jnp.int32, (k, m), 0)

    assert k & (k - 1) == 0 and k <= 2 ** 20

    def body(r, carry):
        x, vals = carry
        mx = jnp.max(x, axis=0, keepdims=True)
        first = jnp.min(jnp.where(x == mx, ridx, float(n)), axis=0, keepdims=True)
        code = (1.0 + lax.convert_element_type(r, F32) * (1.0 / k)) * _PICKED
        x = jnp.where(ridx == first, code, x)
        vals = jnp.where(kidx == r, mx, vals)
        return x, vals

    x, vals = lax.fori_loop(0, k, body, (x, jnp.zeros((k, m), F32)))
    rank = jnp.where(x <= _PICKED, (x * (1.0 / _PICKED) - 1.0) * float(k), float(k))
    return vals, rank


def _peer_select_kernel(h2_ref, wqt_ref, k1_ref, k2_ref, cnt_ref, e1_ref, rk2_ref, e2_ref, qt_s):
    m = h2_ref.shape[0]
    k = P_TOPK
    qt_s[...] = lax.dot_general(wqt_ref[...], h2_ref[...], _NT, preferred_element_type=F32)
    k1 = k1_ref[...].astype(BF16)
    k2 = k2_ref[...].astype(BF16)
    i16 = lax.broadcasted_iota(jnp.int32, (k, m), 0)
    i8 = lax.broadcasted_iota(jnp.int32, (8, m), 0)
    pos = [(i16 * k).astype(F32), (i8 * k + 1).astype(F32)]
    pos += [(i8 * k + jj).astype(F32) for jj in range(2, 8)]
    pos += [(i8 + 8).astype(F32)]
    pos = jnp.concatenate(pos, axis=0)
    ncand = pos.shape[0]

    def per_head(h, _):
        qa = qt_s[pl.ds(pl.multiple_of(h * P_QDIM, P_QDIM), P_HALF), :].astype(BF16)
        qb = qt_s[pl.ds(pl.multiple_of(h * P_QDIM + P_HALF, P_HALF), P_HALF), :].astype(BF16)
        s1 = jnp.dot(k1, qa, preferred_element_type=F32)
        s2 = jnp.dot(k2, qb, preferred_element_type=F32)
        v1, rk1 = _topk_rows(s1, k)
        v2, rk2 = _topk_rows(s2, k)
        blocks = [v1 + v2[0:1], v1[0:8] + v2[1:2]]
        for jj in range(2, 8):
            blocks.append(jnp.where(i8 < k // (jj + 1), v1[0:8] + v2[jj:jj + 1], -jnp.inf))
        blocks.append(v1[0:1] + v2[8:16])
        cand = jnp.concatenate(blocks, axis=0)
        cmax = v1[0:1] + v2[0:1]

        def pick(r, carry):
            cand, z = carry
            mx = jnp.max(cand, axis=0, keepdims=True)
            first = jnp.min(jnp.where(cand == mx, pos, 1e9), axis=0, keepdims=True)
            cand = jnp.where(pos == first, _PICKED, cand)
            return cand, z + jnp.exp(mx - cmax)

        cand, z = lax.fori_loop(0, k, pick, (cand, jnp.zeros((1, m), F32)))
        selm = jnp.where(cand == _PICKED, 1.0, 0.0)
        low = selm[16:24]
        for jj in range(2, 8):
            low = low + selm[8 + 8 * jj:16 + 8 * jj]
        cnt16 = selm[0:16] + jnp.concatenate([low, jnp.zeros((8, m), F32)], axis=0)
        cnt16 = cnt16 + jnp.where(i16 == 0, jnp.sum(selm[72:80], axis=0, keepdims=True), 0.0)
        cnt = jnp.zeros_like(s1)
        for r in range(k):
            cnt = cnt + jnp.where(rk1 == float(r), cnt16[r:r + 1], 0.0)
        cnt_ref[h] = cnt
        e1_ref[h] = jnp.exp(s1 - v1[0:1]) / z
        rk2_ref[h] = rk2.astype(BF16)
        e2_ref[h] = jnp.exp(s2 - v2[0:1]).astype(BF16)
        return 0

    lax.fori_loop(0, P_HEADS, per_head, 0)


def _peer_select(h2, wq_t, k1, k2):
    t, d = h2.shape
    tm = min(t, SELECT_TOKENS)
    sel_spec = pl.BlockSpec((P_HEADS, P_NKEYS, tm), lambda i: (0, 0, i))
    sel_shape = jax.ShapeDtypeStruct((P_HEADS, P_NKEYS, t), F32)
    return pl.pallas_call(
        _peer_select_kernel,
        grid=(t // tm,),
        in_specs=[pl.BlockSpec((tm, d), lambda i: (i, 0)),
                  pl.BlockSpec(wq_t.shape, lambda i: (0, 0)),
                  pl.BlockSpec(k1.shape, lambda i: (0, 0)),
                  pl.BlockSpec(k2.shape, lambda i: (0, 0))],
        out_specs=[sel_spec] * 4,
        out_shape=[sel_shape, sel_shape,
                   jax.ShapeDtypeStruct(sel_shape.shape, BF16), jax.ShapeDtypeStruct(sel_shape.shape, BF16)],
        scratch_shapes=[pltpu.VMEM((P_HEADS * P_QDIM, tm), F32)],
        compiler_params=_params("parallel"),
        name="peer_select",
    )(h2, wq_t, k1, k2)


E_SUB = 8
E_TILE = E_SUB * P_NKEYS


def _gelu(x):
    return 0.5 * x * (1.0 + lax.erf(x * math.sqrt(0.5)))


def _peer_dense_kernel(h2_ref, u_ref, vt_ref, cnt_ref, e1_ref, rk2_ref, e2_ref,
                       x1_ref, ga2_ref, gp2_ref, o_ref, s_s, a_s, acc_s):
    kk = pl.program_id(2)
    bb, lt, d = x1_ref.shape
    m = bb * lt

    @pl.when(kk == 0)
    def _zero():
        acc_s[...] = jnp.zeros_like(acc_s)

    s_s[...] = lax.dot_general(u_ref[...], h2_ref[...], _NT, preferred_element_type=F32)
    for i in range(E_SUB):
        rows = slice(i * P_NKEYS, (i + 1) * P_NKEYS)
        w = jnp.zeros((P_NKEYS, m), BF16)
        for h in range(P_HEADS):
            cnt = jnp.broadcast_to(cnt_ref[h, i:i + 1, :], (P_NKEYS, m)).astype(BF16)
            e1 = jnp.broadcast_to(e1_ref[h, i:i + 1, :], (P_NKEYS, m)).astype(BF16)
            w = w + jnp.where(rk2_ref[h] < cnt, e2_ref[h], jnp.zeros((), BF16)) * e1
        a_s[rows, :] = (w.astype(F32) * _gelu(s_s[rows, :])).astype(BF16)
    acc_s[...] += jnp.dot(vt_ref[...], a_s[...], preferred_element_type=F32)

    @pl.when(kk == pl.num_programs(2) - 1)
    def _finish():
        f = acc_s[...].T
        y = (_rms(f) * gp2_ref[...]).reshape(bb, lt, d)
        o_ref[...] = x1_ref[...] + ga2_ref[...] * y


def _peer_dense(h2, u, v_t, sel, x1, ga2, g_post2, bb, lt):
    b, l, d = x1.shape
    cnt, e1, rk2, e2 = sel
    m = bb * lt
    nl = l // lt
    ne = u.shape[0] // E_TILE
    tok = lambda i, j, kk: i * nl + j
    sub = pl.BlockSpec((P_HEADS, E_SUB, m), lambda i, j, kk: (0, kk, tok(i, j, kk)))
    full = pl.BlockSpec((P_HEADS, P_NKEYS, m), lambda i, j, kk: (0, 0, tok(i, j, kk)))
    xs = pl.BlockSpec((bb, lt, d), lambda i, j, kk: (i, j, 0))
    return pl.pallas_call(
        _peer_dense_kernel,
        grid=(b // bb, nl, ne),
        in_specs=[pl.BlockSpec((m, d), lambda i, j, kk: (tok(i, j, kk), 0)),
                  pl.BlockSpec((E_TILE, d), lambda i, j, kk: (kk, 0)),
                  pl.BlockSpec((d, E_TILE), lambda i, j, kk: (0, kk)),
                  sub, sub, full, full, xs,
                  pl.BlockSpec((bb, 1, d), lambda i, j, kk: (i, 0, 0)),
                  pl.BlockSpec((1, d), lambda i, j, kk: (0, 0))],
        out_specs=xs,
        out_shape=jax.ShapeDtypeStruct((b, l, d), F32),
        scratch_shapes=[pltpu.VMEM((E_TILE, m), F32),
                        pltpu.VMEM((E_TILE, m), BF16),
                        pltpu.VMEM((d, m), F32)],
        compiler_params=_params("parallel", "parallel", "arbitrary"),
        name="peer_dense",
    )(h2, u, v_t, cnt, e1, rk2, e2, x1, ga2, g_post2.reshape(1, d))


def _layer(x, mod, state, prm, bb, lt, chunk, tm_proj):
    b, l, d = x.shape
    t = b * l
    sh1, sc1, ga1, sh2, sc2, ga2 = (mod[:, None, i * d:(i + 1) * d] for i in range(6))

    h = _prenorm(x, sh1, sc1, prm["g_pre1"], bb, lt).reshape(t, d)
    (mparts,) = _multi_matmul(h, [prm["w_m"]], tm_proj, "proj_mlstm")
    q_all, kv0, kv1, kv2 = _multi_matmul(h, [prm["w_q"]] + prm["w_kv"], min(tm_proj, 256), "proj_attn")
    (gates,) = _multi_matmul(h, [prm["w_g"]], tm_proj, "proj_gate")

    mstate = None
    if state is not None:
        mstate = (state[3], state[4], jnp.broadcast_to(state[5][..., None], state[5].shape + (LANE,)),
                  state[6])
    hm, c_new, n_new, m_new, conv_new = _mlstm(
        mparts.reshape(b, l, -1), prm["conv_w"], prm["conv_b"], prm["gate_b"], prm["mh_norm_g"],
        mstate, chunk)

    q_all = q_all.reshape(b, l, -1)
    kvs = [kv.reshape(b, l, 2 * A_WIDTH) for kv in (kv0, kv1, kv2)]
    outs, lses, bufs = [], [], []
    for g in range(N_GROUPS):
        if state is None:
            o, lse = _attn_prompt(q_all, kvs[g], g, PROMPT_HEADS_PER_STEP[g])
            nbuf = _kv_to_cache(kvs[g], min(WINDOWS[g], l))
        else:
            o, lse, nbuf = _attn_sample(q_all, kvs[g], state[g], g)
        outs.append(o)
        lses.append(lse)
        bufs.append(nbuf)

    x1, h2 = _merge(hm, outs, lses, gates.reshape(b, l, 2 * d), x, ga1, sh2, sc2,
                    prm["w_a"], prm["w_b"], prm["w_out"], prm["g_post1"], prm["g_pre2"], bb, lt)
    sel = _peer_select(h2, prm["wq_t"], prm["peer_k1"], prm["peer_k2"])
    y = _peer_dense(h2, prm["peer_u"], prm["peer_vt"], sel, x1, ga2, prm["g_post2"], bb, lt)
    return y, (bufs[0], bufs[1], bufs[2], c_new, n_new, m_new[:, :, 0], conv_new)


def kernel(x_prompt, x_sample, cache_win0_kv, cache_win1_kv, cache_win2_kv, state_mlstm_C, state_mlstm_n, state_mlstm_m, state_conv, c_prompt, c_sample, w_ada, b_ada, g_pre1, g_post1, g_pre2, g_post2, w_in, conv_w, conv_b, b_igate, b_fgate, mh_norm_g, w_a, w_b, w_out, peer_wq, peer_k1, peer_k2, peer_u, peer_v):
    depth = w_in.shape[0]
    d = x_prompt.shape[-1]
    nb_p = x_prompt.shape[0]
    yp, ys = x_prompt, x_sample
    new_p, new_s = [], []
    for layer in range(depth):
        w = w_in[layer]
        o_ig = QK_WIDTH + 2 * M_WIDTH
        o_at = o_ig + 2 * M_HEADS
        o_gt = o_at + 3 * N_GROUPS * A_WIDTH
        prm = {
            "g_pre1": g_pre1[layer], "g_post1": g_post1[layer],
            "g_pre2": g_pre2[layer], "g_post2": g_post2[layer],
            "w_m": jnp.concatenate(
                [w[:, :o_at], jnp.zeros((d, GATE_PAD - 2 * M_HEADS), F32)], axis=1).astype(BF16),
            "w_q": jnp.concatenate(
                [w[:, o_at + 3 * A_WIDTH * g:o_at + 3 * A_WIDTH * g + A_WIDTH] for g in range(N_GROUPS)],
                axis=1).astype(BF16),
            "w_kv": [w[:, o_at + 3 * A_WIDTH * g + A_WIDTH:o_at + 3 * A_WIDTH * (g + 1)].astype(BF16)
                     for g in range(N_GROUPS)],
            "w_g": w[:, o_gt:].astype(BF16),
            "conv_w": conv_w[layer], "conv_b": conv_b[layer],
            "gate_b": jnp.concatenate(
                [b_igate[layer], b_fgate[layer], jnp.zeros((GATE_PAD - 2 * M_HEADS,), F32)]).reshape(1, GATE_PAD),
            "mh_norm_g": mh_norm_g[layer],
            "w_a": w_a[layer].astype(BF16), "w_b": w_b[layer].astype(BF16),
            "w_out": w_out[layer].astype(BF16),
            "wq_t": peer_wq[layer].T.astype(BF16),
            "peer_k1": peer_k1[layer], "peer_k2": peer_k2[layer],
            "peer_u": peer_u[layer].astype(BF16),
            "peer_vt": peer_v[layer].T.astype(BF16),
        }
        mod = _adaln(jnp.concatenate([c_prompt, c_sample], axis=0), w_ada[layer], b_ada[layer])
        state = (cache_win0_kv[layer], cache_win1_kv[layer], cache_win2_kv[layer],
                 state_mlstm_C[layer], state_mlstm_n[layer], state_mlstm_m[layer], state_conv[layer])
        lp = yp.shape[1]
        yp, sp = _layer(yp, mod[:nb_p], None, prm, 1, 512, math.gcd(lp, MLSTM_CHUNK), 512)
        bs, ls = ys.shape[0], ys.shape[1]
        ys, ss = _layer(ys, mod[nb_p:], state, prm, bs, ls, ls, bs * ls)
        new_p.append(sp)
        new_s.append(ss)
    stacked_p = tuple(jnp.stack(z) for z in zip(*new_p))
    stacked_s = tuple(jnp.stack(z) for z in zip(*new_s))
    return (yp, ys) + stacked_p + stacked_s
```
